```python
import math
import jax, jax.numpy as jnp
from jax import lax
import numpy as np

D_MODEL = 1024
BATCH = 8
SEQ = 2048
DEPTH = 1

GDN_HEADS = 8
GDN_DK = 128
GDN_DV = 128
CONV_K = 4
CHUNK = 64
DSA_HEADS = 8
DSA_DH = 128
KV_RANK = 256
IDX_HEADS = 8
IDX_DIM = 64
TOPK_MAX = 256
Q_BLOCK = 128
NUM_BUCKETS = 32
MAX_DISTANCE = 128
EPS = 1e-6

GDN_W = GDN_HEADS * GDN_DV
DSA_W = DSA_HEADS * DSA_DH
SPLITS = (GDN_HEADS * GDN_DK, GDN_HEADS * GDN_DK, GDN_W,
          GDN_W, GDN_HEADS, GDN_HEADS,
          DSA_W, KV_RANK, DSA_W,
          IDX_HEADS * IDX_DIM, IDX_DIM, IDX_HEADS,
          2 * D_MODEL)
PROJ_W = sum(SPLITS)

kernel_name = 'hybrid_gdn_dsa_gated_merge_block'


def _split_points(sizes):
    pts, acc = [], 0
    for s in sizes[:-1]:
        acc += s
        pts.append(acc)
    return pts


def rms_norm(x, g):
    xf = x.astype(jnp.float32)
    y = xf * lax.rsqrt(jnp.mean(xf * xf, axis=-1, keepdims=True) + EPS)
    return (y * g.astype(jnp.float32)).astype(x.dtype)


def l2norm(x):
    return x * lax.rsqrt(jnp.sum(x * x, axis=-1, keepdims=True) + EPS)


def causal_dwconv_silu(u, w):
    C = u.shape[-1]
    y = lax.conv_general_dilated(u, w[:, None, :].astype(u.dtype), window_strides=(1,),
                                 padding=[(CONV_K - 1, 0)],
                                 dimension_numbers=('NWC', 'WIO', 'NWC'),
                                 feature_group_count=C)
    return jax.nn.silu(y)


def gated_delta_rule(q, k, v, g, beta):
    B, T, H, DK = q.shape
    DV = v.shape[-1]
    N = T // CHUNK

    def chunks(a):
        return a.reshape(B, N, CHUNK, H, -1).transpose(0, 3, 1, 2, 4)

    q, k, v = chunks(q), chunks(k), chunks(v)
    g = g.reshape(B, N, CHUNK, H).transpose(0, 3, 1, 2)
    beta = beta.reshape(B, N, CHUNK, H).transpose(0, 3, 1, 2)
    decay = jnp.cumsum(g, axis=-1)
    pos = jnp.arange(CHUNK)
    causal = pos[:, None] >= pos[None, :]
    strict = pos[:, None] > pos[None, :]
    gamma = jnp.exp(jnp.where(causal, decay[..., :, None] - decay[..., None, :], -jnp.inf))
    kb = k * beta[..., None]
    a_mat = jnp.where(strict, jnp.einsum('bhnid,bhnjd->bhnij', kb, k) * gamma, 0.0)
    eye = jnp.eye(CHUNK, dtype=q.dtype)
    rhs = jnp.concatenate([v * beta[..., None], kb * jnp.exp(decay)[..., None]], axis=-1)
    sol = lax.linalg.triangular_solve(a_mat + eye, rhs, left_side=True, lower=True)
    u, w = sol[..., :DV], sol[..., DV:]
    attn = jnp.einsum('bhnid,bhnjd->bhnij', q, k) * gamma
    qg = q * jnp.exp(decay)[..., None]
    d_last = decay[..., -1]
    k_dec = k * jnp.exp(d_last[..., None] - decay)[..., None]

    def step(S, xs):
        u_n, w_n, qg_n, attn_n, kd_n, dl_n = xs
        v_new = u_n - jnp.einsum('bhcd,bhde->bhce', w_n, S)
        o = jnp.einsum('bhcd,bhde->bhce', qg_n, S) + jnp.einsum('bhij,bhje->bhie', attn_n, v_new)
        S = S * jnp.exp(dl_n)[..., None, None] + jnp.einsum('bhcd,bhce->bhde', kd_n, v_new)
        return S, o

    xs = tuple(jnp.moveaxis(a, 2, 0) for a in (u, w, qg, attn, k_dec, d_last))
    S0 = jnp.zeros((B, H, DK, DV), q.dtype)
    _, o = lax.scan(step, S0, xs)
    return o.transpose(1, 0, 3, 2, 4).reshape(B, T, H, DV)


def t5_bucket(dist):
    n = jnp.maximum(dist, 0)
    max_exact = NUM_BUCKETS // 2
    nf = jnp.maximum(n, 1).astype(jnp.float32)
    large = max_exact + (jnp.log(nf / max_exact) / math.log(MAX_DISTANCE / max_exact)
                         * (NUM_BUCKETS - max_exact)).astype(jnp.int32)
    large = jnp.minimum(large, NUM_BUCKETS - 1)
    return jnp.where(n < max_exact, n, large)


def dsa_attention(q_lat, c_kv, q_idx, k_idx, w_idx, rel_bias):
    B, T, H, DC = q_lat.shape
    S = c_kv.shape[1]
    k_sel = min(TOPK_MAX, S // 4)
    nb = T // Q_BLOCK
    f32 = jnp.float32

    def blocks(a):
        return jnp.moveaxis(a.reshape(B, nb, Q_BLOCK, *a.shape[2:]), 1, 0)

    key_pos = jnp.arange(S)
    k_idx32 = k_idx.astype(f32)

    def one_block(xs):
        ql, qi, wi, t0 = xs
        t = t0 + jnp.arange(Q_BLOCK)
        dots = jnp.einsum('bqhd,bsd->bqhs', qi.astype(f32), k_idx32)
        score = jnp.einsum('bqh,bqhs->bqs', wi.astype(f32), jax.nn.relu(dots))
        admissible = key_pos[None, :] <= t[:, None]
        score = jnp.where(admissible[None], score, -jnp.inf)
        _, sel = lax.top_k(score, k_sel)
        valid = sel <= t[None, :, None]
        kv = jax.vmap(lambda ck, ii: ck[ii])(c_kv, sel)
        logits = jnp.einsum('bqhc,bqkc->bqhk', ql, kv).astype(f32)
        bias = rel_bias[t5_bucket(t[None, :, None] - sel)]
        logits = logits + jnp.moveaxis(bias, -1, 2).astype(f32)
        logits = jnp.where(valid[:, :, None, :], logits, -jnp.inf)
        p = jax.nn.softmax(logits, axis=-1).astype(kv.dtype)
        return jnp.einsum('bqhk,bqkc->bqhc', p, kv)

    t0s = jnp.arange(nb) * Q_BLOCK
    o = lax.map(one_block, (blocks(q_lat), blocks(q_idx), blocks(w_idx), t0s))
    return jnp.moveaxis(o, 0, 1).reshape(B, T, H, DC)


def setup_inputs(seed: int = 0) -> dict:
    key = jax.random.key(seed)
    ks = jax.random.split(key, 20)
    nrm = jax.random.normal
    D = D_MODEL
    x = nrm(ks[0], (BATCH, SEQ, D), jnp.float32)
    c = nrm(ks[1], (BATCH, D), jnp.float32)
    norm_g = 1.0 + 0.02 * nrm(ks[2], (DEPTH, D), jnp.float32)
    w_ada = nrm(ks[3], (DEPTH, D, 3 * D), jnp.float32) * D ** -0.5
    b_ada = 0.01 * nrm(ks[4], (DEPTH, 3 * D), jnp.float32)
    w_in = nrm(ks[5], (DEPTH, D, PROJ_W), jnp.float32) * D ** -0.5
    conv_w = nrm(ks[6], (DEPTH, CONV_K, 2 * GDN_HEADS * GDN_DK + GDN_W), jnp.float32) * CONV_K ** -0.5
    a_log = jnp.log(jax.random.uniform(ks[7], (DEPTH, GDN_HEADS), jnp.float32, 1.0, 16.0))
    dt = jnp.exp(jax.random.uniform(ks[8], (DEPTH, GDN_HEADS), jnp.float32,
                                    math.log(1e-3), math.log(1e-1)))
    dt_bias = jnp.log(jnp.expm1(dt))
    gdn_norm_g = 1.0 + 0.02 * nrm(ks[9], (DEPTH, GDN_DV), jnp.float32)
    w_o_gdn = nrm(ks[10], (DEPTH, GDN_W, D), jnp.float32) * GDN_W ** -0.5
    kv_norm_g = 1.0 + 0.02 * nrm(ks[11], (DEPTH, KV_RANK), jnp.float32)
    w_uk = nrm(ks[12], (DEPTH, KV_RANK, DSA_HEADS, DSA_DH), jnp.float32) * KV_RANK ** -0.5
    w_uv = nrm(ks[13], (DEPTH, KV_RANK, DSA_HEADS, DSA_DH), jnp.float32) * KV_RANK ** -0.5
    w_o_dsa = nrm(ks[14], (DEPTH, DSA_W, D), jnp.float32) * DSA_W ** -0.5
    w_out = nrm(ks[15], (DEPTH, D, D), jnp.float32) * D ** -0.5
    rel_bias = 0.5 * nrm(ks[16], (NUM_BUCKETS, DSA_HEADS), jnp.float32)
    final_g = 1.0 + 0.02 * nrm(ks[17], (D,), jnp.float32)
    return {'x': x, 'c': c, 'norm_g': norm_g, 'w_ada': w_ada, 'b_ada': b_ada, 'w_in': w_in,
            'conv_w': conv_w, 'a_log': a_log, 'dt_bias': dt_bias, 'gdn_norm_g': gdn_norm_g,
            'w_o_gdn': w_o_gdn, 'kv_norm_g': kv_norm_g, 'w_uk': w_uk, 'w_uv': w_uv,
            'w_o_dsa': w_o_dsa, 'w_out': w_out, 'rel_bias': rel_bias, 'final_g': final_g}


def reference(x, c, norm_g, w_ada, b_ada, w_in, conv_w, a_log, dt_bias, gdn_norm_g, w_o_gdn,
              kv_norm_g, w_uk, w_uv, w_o_dsa, w_out, rel_bias, final_g):
    B, T, D = x.shape
    f32 = jnp.float32
    pts = _split_points(SPLITS)
    qkv_pts = [GDN_HEADS * GDN_DK, 2 * GDN_HEADS * GDN_DK]
    for l in range(DEPTH):
        mod = jax.nn.silu(c) @ w_ada[l] + b_ada[l]
        shift, scale, gate = jnp.split(mod, 3, axis=-1)
        h = rms_norm(x, norm_g[l]) * (1.0 + scale[:, None]) + shift[:, None]
        proj = h @ w_in[l]
        (qa, ka, va, za, aa, ba, qb, ckv, zb, qi, ki, wi, gl) = jnp.split(proj, pts, axis=-1)

        qkv = causal_dwconv_silu(jnp.concatenate([qa, ka, va], axis=-1), conv_w[l])
        qa, ka, va = jnp.split(qkv, qkv_pts, axis=-1)
        qa = l2norm(qa.reshape(B, T, GDN_HEADS, GDN_DK).astype(f32)) * GDN_DK ** -0.5
        ka = l2norm(ka.reshape(B, T, GDN_HEADS, GDN_DK).astype(f32))
        va = va.reshape(B, T, GDN_HEADS, GDN_DV).astype(f32)
        beta = jax.nn.sigmoid(ba.astype(f32))
        g = -jnp.exp(a_log[l].astype(f32)) * jax.nn.softplus(aa.astype(f32) + dt_bias[l].astype(f32))
        oa = gated_delta_rule(qa, ka, va, g, beta)
        oa = rms_norm(oa, gdn_norm_g[l]).astype(x.dtype).reshape(B, T, GDN_W) * jax.nn.silu(za)
        ya = oa @ w_o_gdn[l]

        ckv = rms_norm(ckv, kv_norm_g[l])
        qb = qb.reshape(B, T, DSA_HEADS, DSA_DH)
        q_lat = jnp.einsum('bthd,chd->bthc', qb, w_uk[l]) * DSA_DH ** -0.5
        qi = qi.reshape(B, T, IDX_HEADS, IDX_DIM)
        wi = wi * (IDX_HEADS ** -0.5 * IDX_DIM ** -0.5)
        o_lat = dsa_attention(q_lat, ckv, qi, ki, wi, rel_bias)
        ob = jnp.einsum('bthc,chd->bthd', o_lat, w_uv[l]).reshape(B, T, DSA_W) * jax.nn.silu(zb)
        yb = ob @ w_o_dsa[l]

        ga, gb = jnp.split(jax.nn.sigmoid(gl), 2, axis=-1)
        out = (ga * ya + gb * yb) @ w_out[l]
        x = x + gate[:, None] * out
    return rms_norm(x, final_g)
```

```python
import functools
import math

import numpy as np
import jax
import jax.numpy as jnp
from jax import lax
from jax.experimental import pallas as pl
from jax.experimental.pallas import tpu as pltpu

D_MODEL = 1024
GDN_HEADS = 8
GDN_DK = 128
GDN_DV = 128
CONV_K = 4
DSA_HEADS = 8
DSA_DH = 128
KV_RANK = 256
IDX_HEADS = 8
IDX_DIM = 64
TOPK_MAX = 256
NUM_BUCKETS = 32
MAX_DISTANCE = 128
EPS = 1e-6

LANES = 128
GDN_W = GDN_HEADS * GDN_DV
DSA_W = DSA_HEADS * DSA_DH

F32 = jnp.float32
BF16 = jnp.bfloat16
INT_MIN = -2 ** 31
VMEM_LIMIT = 48 * 1024 * 1024

_NT = (((1,), (1,)), ((), ()))
_TN = (((0,), (0,)), ((), ()))


def _dot(a, b):
    return jnp.dot(a, b, preferred_element_type=F32)


def _dot_nt(a, b):
    return lax.dot_general(a, b, _NT, preferred_element_type=F32)


def _dot_tn(a, b):
    return lax.dot_general(a, b, _TN, preferred_element_type=F32)


def _split2(a):
    hi = a.astype(BF16)
    lo = (a - hi.astype(F32)).astype(BF16)
    return hi, lo


def _mm3(a, b):
    ah, al = _split2(a)
    bh, bl = _split2(b)
    return _dot(ah, bh) + (_dot(ah, bl) + _dot(al, bh))


def _sigmoid(x):
    return 1.0 / (1.0 + jnp.exp(-x))


def _silu(x):
    return x * _sigmoid(x)


def _softplus(x):
    return jnp.maximum(x, 0.0) + jnp.log1p(jnp.exp(-jnp.abs(x)))


def _mod_kernel(c_ref, w_ref, b_ref, o_ref):
    a = _silu(c_ref[...])
    o_ref[...] = _mm3(a, w_ref[...]) + b_ref[...]


def _adaln_mod(c, w_ada, b_ada):
    bsz, d = c.shape
    n = w_ada.shape[1]
    tn = 512
    return pl.pallas_call(
        _mod_kernel,
        grid=(n // tn,),
        in_specs=[pl.BlockSpec((bsz, d), lambda j: (0, 0)),
                  pl.BlockSpec((d, tn), lambda j: (0, j)),
                  pl.BlockSpec((1, tn), lambda j: (0, j))],
        out_specs=pl.BlockSpec((bsz, tn), lambda j: (0, j)),
        out_shape=jax.ShapeDtypeStruct((bsz, n), F32),
        name="adaln_mod",
    )(c, w_ada, b_ada.reshape(1, n))


def _h_kernel(x_ref, g_ref, sc_ref, sh_ref, o_ref):
    x = x_ref[...]
    y = x * lax.rsqrt(jnp.mean(x * x, axis=-1, keepdims=True) + EPS)
    y = y * g_ref[...]
    o_ref[...] = (y * (1.0 + sc_ref[0]) + sh_ref[0]).astype(o_ref.dtype)


def _modulated_norm(x2, norm_g, scale, shift, seq):
    m, d = x2.shape
    bsz = m // seq
    tm = 512
    nt = seq // tm
    return pl.pallas_call(
        _h_kernel,
        grid=(bsz, nt),
        in_specs=[pl.BlockSpec((tm, d), lambda b, i: (b * nt + i, 0)),
                  pl.BlockSpec((1, d), lambda b, i: (0, 0)),
                  pl.BlockSpec((1, 1, d), lambda b, i: (b, 0, 0)),
                  pl.BlockSpec((1, 1, d), lambda b, i: (b, 0, 0))],
        out_specs=pl.BlockSpec((tm, d), lambda b, i: (b * nt + i, 0)),
        out_shape=jax.ShapeDtypeStruct((m, d), BF16),
        name="modulated_norm",
    )(x2, norm_g.reshape(1, d), scale.reshape(bsz, 1, d), shift.reshape(bsz, 1, d))


def _proj_kernel(h_ref, w_ref, o_ref):
    o_ref[...] = _dot(h_ref[...], w_ref[...]).astype(o_ref.dtype)


def _project(h, w, out_dtype, tn, tm=1024):
    m, k = h.shape
    n = w.shape[1]
    return pl.pallas_call(
        _proj_kernel,
        grid=(m // tm, n // tn),
        in_specs=[pl.BlockSpec((tm, k), lambda i, j: (i, 0)),
                  pl.BlockSpec((k, tn), lambda i, j: (0, j))],
        out_specs=pl.BlockSpec((tm, tn), lambda i, j: (i, j)),
        out_shape=jax.ShapeDtypeStruct((m, n), out_dtype),
        compiler_params=pltpu.CompilerParams(vmem_limit_bytes=VMEM_LIMIT),
        name="in_proj",
    )(h, w)


def _unit_lower_inverse(a):
    n = a.shape[0]
    ri = lax.broadcasted_iota(jnp.int32, (n, n), 0)
    ci = lax.broadcasted_iota(jnp.int32, (n, n), 1)
    x = ri ^ ci
    eye = (ri == ci).astype(F32)
    t = eye - jnp.where(x == 1, a, 0.0)
    s = 2
    while s < n:
        off = jnp.where((x >= s) & (x < 2 * s), a, 0.0)
        t = t - _mm3(t, _mm3(off, t))
        s *= 2
    return t


def _gdn_kernel(q_ref, k_ref, v_ref, z_ref, misc_ref, cwq_ref, cwk_ref, cwv_ref,
                alog_ref, dtb_ref, ng_ref, o_ref, qs, ks, vs, gs, bs):
    hd = pl.program_id(1)
    seq = q_ref.shape[0]
    c = LANES
    row = lax.broadcasted_iota(jnp.int32, (seq, c), 0)
    lane = lax.broadcasted_iota(jnp.int32, (seq, c), 1)

    def conv_silu(u_ref, w_ref):
        u = u_ref[...]
        w = w_ref[...]
        y = u * w[CONV_K - 1:CONV_K, :]
        for k in range(1, CONV_K):
            sh = jnp.where(row >= k, pltpu.roll(u, k, 0), 0.0)
            y = y + sh * w[CONV_K - 1 - k:CONV_K - k, :]
        return _silu(y)

    def l2n(x):
        return x * lax.rsqrt(jnp.sum(x * x, axis=-1, keepdims=True) + EPS)

    qs[...] = l2n(conv_silu(q_ref, cwq_ref)) * (GDN_DK ** -0.5)
    ks[...] = l2n(conv_silu(k_ref, cwk_ref))
    vs[...] = conv_silu(v_ref, cwv_ref)

    misc = misc_ref[...]
    aa = jnp.sum(jnp.where(lane == hd, misc, 0.0), axis=-1, keepdims=True)
    ba = jnp.sum(jnp.where(lane == hd + GDN_HEADS, misc, 0.0), axis=-1, keepdims=True)
    gs[...] = -jnp.exp(alog_ref[0]) * _softplus(aa + dtb_ref[0])
    bs[...] = jnp.broadcast_to(_sigmoid(ba), (seq, c))

    ri = lax.broadcasted_iota(jnp.int32, (c, c), 0)
    ci = lax.broadcasted_iota(jnp.int32, (c, c), 1)
    tri = (ri >= ci).astype(BF16)
    ng = ng_ref[...]

    def chunk(n, state):
        sl = pl.ds(pl.multiple_of(n * c, c), c)
        q = qs[sl, :]
        k = ks[sl, :]
        v = vs[sl, :]
        g = gs[sl, :]
        beta = bs[sl, :]
        g1 = g.astype(BF16)
        r1 = g - g1.astype(F32)
        g2 = r1.astype(BF16)
        g3 = (r1 - g2.astype(F32)).astype(BF16)
        dcol = _dot(tri, g1) + (_dot(tri, g2) + _dot(tri, g3))
        drow = dcol.T
        dlast = dcol[c - 1:c, :]
        ecol = jnp.exp(dcol)
        gamma = jnp.exp(jnp.where(ri >= ci, dcol - drow, -jnp.inf))
        kb = k * beta
        kbf = k.astype(BF16)
        a = jnp.where(ri > ci, _dot_nt(kb.astype(BF16), kbf) * gamma, 0.0)
        tinv = _unit_lower_inverse(a)
        u = _mm3(tinv, v * beta)
        w = _mm3(tinv, kb * ecol)
        attn = _dot_nt(q.astype(BF16), kbf) * gamma
        qg = q * ecol
        kdec = k * jnp.exp(dlast - dcol)
        sb = state.astype(BF16)
        v_new = u - _dot(w.astype(BF16), sb)
        o = _dot(qg.astype(BF16), sb) + _dot(attn.astype(BF16), v_new.astype(BF16))
        state = state * jnp.exp(dlast) + _dot_tn(kdec.astype(BF16), v_new.astype(BF16))
        on = o * lax.rsqrt(jnp.mean(o * o, axis=-1, keepdims=True) + EPS) * ng
        o_ref[sl, :] = (on * _silu(z_ref[sl, :])).astype(o_ref.dtype)
        return state

    lax.fori_loop(0, seq // c, chunk, jnp.zeros((GDN_DK, GDN_DV), F32))


def _gdn(pa, pb, conv_w, a_log, dt_bias, gdn_norm_g, bsz, seq):
    nh = GDN_HEADS
    c = LANES
    alog_b = jnp.broadcast_to(a_log.reshape(nh, 1, 1), (nh, 1, c))
    dtb_b = jnp.broadcast_to(dt_bias.reshape(nh, 1, 1), (nh, 1, c))
    blk = lambda off: pl.BlockSpec((seq, c), lambda b, h: (b, off + h))
    cw = lambda off: pl.BlockSpec((CONV_K, c), lambda b, h: (0, off + h))
    hv = pl.BlockSpec((1, 1, c), lambda b, h: (h, 0, 0))
    return pl.pallas_call(
        _gdn_kernel,
        grid=(bsz, nh),
        in_specs=[blk(0), blk(nh), blk(2 * nh), blk(3 * nh),
                  pl.BlockSpec((seq, c), lambda b, h: (b, 2)),
                  cw(0), cw(nh), cw(2 * nh), hv, hv,
                  pl.BlockSpec((1, c), lambda b, h: (0, 0))],
        out_specs=pl.BlockSpec((seq, c), lambda b, h: (b, h)),
        out_shape=jax.ShapeDtypeStruct((bsz * seq, GDN_W), BF16),
        scratch_shapes=[pltpu.VMEM((seq, c), F32)] * 5,
        compiler_params=pltpu.CompilerParams(vmem_limit_bytes=VMEM_LIMIT),
        name="gated_deltanet",
    )(pa, pa, pa, pa, pb, conv_w, conv_w, conv_w, alog_b, dtb_b,
      gdn_norm_g.reshape(1, c))


def _t5_bucket_table(n_tiles):
    q = np.arange(LANES)[:, None]
    s = np.arange(LANES)[None, :]
    out = []
    max_exact = NUM_BUCKETS // 2
    for d in range(n_tiles):
        n = np.maximum(d * LANES + q - s, 0)
        nf = np.maximum(n, 1).astype(np.float32)
        large = max_exact + (np.log(nf / max_exact) / math.log(MAX_DISTANCE / max_exact)
                             * (NUM_BUCKETS - max_exact)).astype(np.int32)
        large = np.minimum(large, NUM_BUCKETS - 1)
        out.append(np.where(n < max_exact, n, large))
    return np.stack(out).astype(np.int32)


_BIAS_TILES = 3
assert (_BIAS_TILES - 1) * LANES - (LANES - 1) >= MAX_DISTANCE


def _bias_kernel(bkt_ref, rel_ref, o_ref):
    hd = pl.program_id(0)
    for d in range(_BIAS_TILES):
        bkt = bkt_ref[d]
        acc = jnp.zeros((LANES, LANES), F32)
        for j in range(NUM_BUCKETS):
            acc = jnp.where(bkt == j, rel_ref[j, hd], acc)
        o_ref[d, 0] = acc


def _bias_tiles(rel_bias):
    bkt = jnp.asarray(_t5_bucket_table(_BIAS_TILES))
    return pl.pallas_call(
        _bias_kernel,
        grid=(DSA_HEADS,),
        in_specs=[pl.BlockSpec((_BIAS_TILES, LANES, LANES), lambda h: (0, 0, 0)),
                  pl.BlockSpec(memory_space=pltpu.SMEM)],
        out_specs=pl.BlockSpec((_BIAS_TILES, 1, LANES, LANES), lambda h: (0, h, 0, 0)),
        out_shape=jax.ShapeDtypeStruct((_BIAS_TILES, DSA_HEADS, LANES, LANES), F32),
        name="t5_bias_tiles",
    )(bkt, rel_bias)


def _dsa_kernel(qb_ref, qi_ref, misc_ref, zb_ref, kil_ref, kih_ref, ckv_ref,
                kvg_ref, wuk_ref, wuv_ref, bias_ref, o_ref,
                ckvn, keys, madd, lg):
    i = pl.program_id(1)
    c = LANES
    nck = i + 1
    seq = ckv_ref.shape[0]

    @pl.when(i == 0)
    def _():
        x = ckv_ref[...]
        y = x * lax.rsqrt(jnp.mean(x * x, axis=-1, keepdims=True) + EPS)
        ckvn[...] = (y * kvg_ref[...]).astype(BF16)

    ri = lax.broadcasted_iota(jnp.int32, (c, c), 0)
    ci = lax.broadcasted_iota(jnp.int32, (c, c), 1)
    misc = misc_ref[...]
    wscale = IDX_HEADS ** -0.5 * IDX_DIM ** -0.5
    wcols = [misc[:, 2 * GDN_HEADS + h:2 * GDN_HEADS + h + 1] * wscale
             for h in range(IDX_HEADS)]
    qi = qi_ref[...]

    def score_chunk(cc, carry):
        sl = pl.ds(pl.multiple_of(cc * c, c), c)
        kl = kil_ref[sl, :]
        kh = kih_ref[sl, :]
        acc = jnp.zeros((c, c), F32)
        for h in range(IDX_HEADS):
            qpair = qi[:, (h // 2) * c:(h // 2 + 1) * c]
            d = _dot_nt(qpair, kl if h % 2 == 0 else kh)
            acc = acc + wcols[h] * jnp.maximum(d, 0.0)
        adm = (cc * c + ci) <= (i * c + ri)
        acc = jnp.where(adm, acc, -jnp.inf)
        bits = lax.bitcast_convert_type(acc, jnp.int32)
        keys[cc] = jnp.where(bits < 0, bits ^ 0x7FFFFFFF, bits)
        return carry

    lax.fori_loop(0, nck, score_chunk, 0)

    kf = float(TOPK_MAX)

    def count_ge(cand):
        def body(cc, cnt):
            return cnt + jnp.where(keys[cc] >= cand, 1.0, 0.0)
        cnt = lax.fori_loop(0, nck, body, jnp.zeros((c, c), F32))
        return jnp.sum(cnt, axis=-1, keepdims=True)

    def bit_step(it, v):
        step = jnp.left_shift(jnp.int32(1), 31 - it)
        cand = v + step
        tot = count_ge(cand)
        return jnp.where(tot >= kf, cand, v)

    thr = lax.fori_loop(0, 32, bit_step, jnp.full((c, c), INT_MIN, jnp.int32))

    def gt_body(cc, cnt):
        return cnt + jnp.where(keys[cc] > thr, 1.0, 0.0)
    n_gt = jnp.sum(lax.fori_loop(0, nck, gt_body, jnp.zeros((c, c), F32)),
                   axis=-1, keepdims=True)
    need = kf - n_gt
    upper = (ri < ci).astype(BF16)

    def mask_chunk(cc, running):
        kc = keys[cc]
        eq = jnp.where(kc == thr, 1.0, 0.0)
        rank = running + _dot(eq.astype(BF16), upper)
        sel = jnp.where(kc > thr, 1.0, jnp.where(rank < need, eq, 0.0))
        adm = (cc * c + ci) <= (i * c + ri)
        madd[cc] = jnp.where(adm & (sel > 0.5), 0.0, -jnp.inf)
        return running + jnp.sum(eq, axis=-1, keepdims=True)

    lax.fori_loop(0, nck, mask_chunk, jnp.zeros((c, 1), F32))

    for h in range(DSA_HEADS):
        qh = qb_ref[:, h * DSA_DH:(h + 1) * DSA_DH]
        qlat = (_dot_nt(qh, wuk_ref[h]) * (DSA_DH ** -0.5)).astype(BF16)

        def logit_chunk(cc, mx, qlat=qlat, h=h):
            sl = pl.ds(pl.multiple_of(cc * c, c), c)
            l = _dot_nt(qlat, ckvn[sl, :])
            l = l + bias_ref[jnp.minimum(i - cc, _BIAS_TILES - 1), h] + madd[cc]
            lg[cc] = l
            return jnp.maximum(mx, l)

        mx = lax.fori_loop(0, nck, logit_chunk, jnp.full((c, c), -jnp.inf, F32))
        m = jnp.max(mx, axis=-1, keepdims=True)

        def pv_chunk(cc, carry, m=m):
            acc, ssum = carry
            sl = pl.ds(pl.multiple_of(cc * c, c), c)
            p = jnp.exp(lg[cc] - m)
            acc = acc + _dot(p.astype(BF16), ckvn[sl, :])
            return acc, ssum + p

        acc, ssum = lax.fori_loop(
            0, nck, pv_chunk,
            (jnp.zeros((c, KV_RANK), F32), jnp.zeros((c, c), F32)))
        olat = acc / jnp.sum(ssum, axis=-1, keepdims=True)
        ob = _dot(olat.astype(BF16), wuv_ref[h])
        zb = zb_ref[:, h * DSA_DH:(h + 1) * DSA_DH]
        o_ref[:, h * DSA_DH:(h + 1) * DSA_DH] = (ob * _silu(zb)).astype(o_ref.dtype)


def _dsa(pa, pb, pc, kv_norm_g, wuk_t, wuv_t, bias_tiles, bsz, seq):
    c = LANES
    nb = seq // c
    za_off = (4 * GDN_W + 2 * D_MODEL) // DSA_W
    qi_off = DSA_W // (IDX_HEADS * IDX_DIM)
    kil_off = (DSA_W + IDX_HEADS * IDX_DIM) // c
    row = lambda b, i: b * nb + i
    return pl.pallas_call(
        _dsa_kernel,
        grid=(bsz, nb),
        in_specs=[
            pl.BlockSpec((c, DSA_W), lambda b, i: (row(b, i), 0)),
            pl.BlockSpec((c, IDX_HEADS * IDX_DIM), lambda b, i: (row(b, i), qi_off)),
            pl.BlockSpec((c, c), lambda b, i: (row(b, i), 2)),
            pl.BlockSpec((c, DSA_W), lambda b, i: (row(b, i), za_off)),
            pl.BlockSpec((seq, c), lambda b, i: (b, kil_off)),
            pl.BlockSpec((seq, c), lambda b, i: (b, kil_off + 1)),
            pl.BlockSpec((seq, KV_RANK), lambda b, i: (b, 0)),
            pl.BlockSpec((1, KV_RANK), lambda b, i: (0, 0)),
            pl.BlockSpec((DSA_HEADS, KV_RANK, DSA_DH), lambda b, i: (0, 0, 0)),
            pl.BlockSpec((DSA_HEADS, KV_RANK, DSA_DH), lambda b, i: (0, 0, 0)),
            pl.BlockSpec((_BIAS_TILES, DSA_HEADS, c, c), lambda b, i: (0, 0, 0, 0)),
        ],
        out_specs=pl.BlockSpec((c, DSA_W), lambda b, i: (row(b, i), 0)),
        out_shape=jax.ShapeDtypeStruct((bsz * seq, DSA_W), BF16),
        scratch_shapes=[pltpu.VMEM((seq, KV_RANK), BF16),
                        pltpu.VMEM((nb, c, c), jnp.int32),
                        pltpu.VMEM((nb, c, c), F32),
                        pltpu.VMEM((nb, c, c), F32)],
        compiler_params=pltpu.CompilerParams(vmem_limit_bytes=VMEM_LIMIT),
        name="dsa_attention",
    )(pc, pc, pb, pa, pc, pc, pb, kv_norm_g.reshape(1, KV_RANK), wuk_t, wuv_t,
      bias_tiles)


def _out_kernel(oa_ref, ob_ref, gl_ref, x_ref, gate_ref, woa_ref, wob_ref, wout_ref,
                fg_ref, o_ref):
    d = x_ref.shape[1]
    ya = _dot(oa_ref[...], woa_ref[...])
    yb = _dot(ob_ref[...], wob_ref[...])
    gl = gl_ref[...]
    mix = _sigmoid(gl[:, :d]) * ya + _sigmoid(gl[:, d:]) * yb
    out = _dot(mix.astype(BF16), wout_ref[...])
    xn = x_ref[...] + gate_ref[0] * out
    y = xn * lax.rsqrt(jnp.mean(xn * xn, axis=-1, keepdims=True) + EPS)
    o_ref[...] = y * fg_ref[...]


def _merge_out(oa, ob, pa, x2, gate, w_o_gdn, w_o_dsa, w_out, final_g, bsz, seq):
    d = D_MODEL
    tm = 512
    nt = seq // tm
    gl_off = (4 * GDN_W) // (2 * d)
    row = lambda b, i: (b * nt + i, 0)
    full = lambda b, i: (0, 0)
    return pl.pallas_call(
        _out_kernel,
        grid=(bsz, nt),
        in_specs=[pl.BlockSpec((tm, GDN_W), row),
                  pl.BlockSpec((tm, DSA_W), row),
                  pl.BlockSpec((tm, 2 * d), lambda b, i: (b * nt + i, gl_off)),
                  pl.BlockSpec((tm, d), row),
                  pl.BlockSpec((1, 1, d), lambda b, i: (b, 0, 0)),
                  pl.BlockSpec((GDN_W, d), full),
                  pl.BlockSpec((DSA_W, d), full),
                  pl.BlockSpec((d, d), full),
                  pl.BlockSpec((1, d), full)],
        out_specs=pl.BlockSpec((tm, d), row),
        out_shape=jax.ShapeDtypeStruct((bsz * seq, d), F32),
        compiler_params=pltpu.CompilerParams(vmem_limit_bytes=VMEM_LIMIT),
        name="merge_out",
    )(oa, ob, pa, x2, gate.reshape(bsz, 1, d), w_o_gdn, w_o_dsa, w_out,
      final_g.reshape(1, d))


def _split_w_in(w):
    o = 0
    names = ("qa", "ka", "va", "za", "aa", "ba", "qb", "ckv", "zb", "qi", "ki", "wi", "gl")
    sizes = (GDN_HEADS * GDN_DK, GDN_HEADS * GDN_DK, GDN_W, GDN_W, GDN_HEADS, GDN_HEADS,
             DSA_W, KV_RANK, DSA_W, IDX_HEADS * IDX_DIM, IDX_DIM, IDX_HEADS, 2 * D_MODEL)
    col = {}
    for nm, sz in zip(names, sizes):
        col[nm] = w[:, o:o + sz]
        o += sz
    k = w.shape[0]
    z = lambda n: jnp.zeros((k, n), w.dtype)
    wa = jnp.concatenate([col["qa"], col["ka"], col["va"], col["za"], col["gl"], col["zb"]], 1)
    n_misc = 2 * GDN_HEADS + IDX_HEADS
    wb = jnp.concatenate([col["ckv"], col["aa"], col["ba"], col["wi"], z(LANES - n_misc)], 1)
    pad = z(LANES - IDX_DIM)
    wc = jnp.concatenate([col["qb"], col["qi"], col["ki"], pad, pad, col["ki"]], 1)
    return wa.astype(BF16), wb.astype(BF16), wc.astype(BF16)


def kernel(x, c, norm_g, w_ada, b_ada, w_in, conv_w, a_log, dt_bias, gdn_norm_g, w_o_gdn,
           kv_norm_g, w_uk, w_uv, w_o_dsa, w_out, rel_bias, final_g):
    bsz, seq, d = x.shape
    depth = norm_g.shape[0]
    assert depth == 1, "the final RMSNorm is fused into the layer's merge kernel"
    x2 = x.reshape(bsz * seq, d)
    bias_tiles = _bias_tiles(rel_bias)
    for l in range(depth):
        mod = _adaln_mod(c, w_ada[l], b_ada[l])
        shift, scale, gate = mod[:, :d], mod[:, d:2 * d], mod[:, 2 * d:]
        h = _modulated_norm(x2, norm_g[l], scale, shift, seq)
        wa, wb, wc = _split_w_in(w_in[l])
        pa = _project(h, wa, F32, 512)
        pb = _project(h, wb, F32, wb.shape[1])
        pc = _project(h, wc, BF16, wc.shape[1] // 2)
        oa = _gdn(pa, pb, conv_w[l], a_log[l], dt_bias[l], gdn_norm_g[l], bsz, seq)
        wuk_t = jnp.transpose(w_uk[l], (1, 0, 2)).astype(BF16)
        wuv_t = jnp.transpose(w_uv[l], (1, 0, 2)).astype(BF16)
        ob = _dsa(pa, pb, pc, kv_norm_g[l], wuk_t, wuv_t, bias_tiles, bsz, seq)
        x2 = _merge_out(oa, ob, pa, x2, gate, w_o_gdn[l].astype(BF16),
                        w_o_dsa[l].astype(BF16), w_out[l].astype(BF16), final_g,
                        bsz, seq)
    return x2.reshape(bsz, seq, d)
```

```python
import math

import numpy as np
import jax
import jax.numpy as jnp
from jax import lax
from jax.experimental import pallas as pl
from jax.experimental.pallas import tpu as pltpu

D_MODEL = 1024
GDN_HEADS = 8
GDN_DK = 128
GDN_DV = 128
CONV_K = 4
DSA_HEADS = 8
DSA_DH = 128
KV_RANK = 256
IDX_HEADS = 8
IDX_DIM = 64
TOPK_MAX = 256
NUM_BUCKETS = 32
MAX_DISTANCE = 128
EPS = 1e-6

LANES = 128
GDN_GROUP = 4
DSA_SUPER = 4
GDN_W = GDN_HEADS * GDN_DV
DSA_W = DSA_HEADS * DSA_DH

F32 = jnp.float32
BF16 = jnp.bfloat16
INT_MIN = -2 ** 31
VMEM_LIMIT = 48 * 1024 * 1024

_NT = (((1,), (1,)), ((), ()))


def _dot(a, b):
    return jnp.dot(a, b, preferred_element_type=F32)


def _dot_nt(a, b):
    return lax.dot_general(a, b, _NT, preferred_element_type=F32)


def _split2(a):
    hi = a.astype(BF16)
    lo = (a - hi.astype(F32)).astype(BF16)
    return hi, lo


def _mm3(a, b):
    ah, al = _split2(a)
    bh, bl = _split2(b)
    return _dot(ah, bh) + (_dot(ah, bl) + _dot(al, bh))


def _sigmoid(x):
    return 1.0 / (1.0 + jnp.exp(-x))


def _silu(x):
    return x * _sigmoid(x)


def _softplus(x):
    return jnp.maximum(x, 0.0) + jnp.log1p(jnp.exp(-jnp.abs(x)))


def _mod_kernel(c_ref, w_ref, b_ref, o_ref):
    a = _silu(c_ref[...])
    o_ref[...] = _mm3(a, w_ref[...]) + b_ref[...]


def _adaln_mod(c, w_ada, b_ada):
    bsz, d = c.shape
    n = w_ada.shape[1]
    tn = 512
    return pl.pallas_call(
        _mod_kernel,
        grid=(n // tn,),
        in_specs=[pl.BlockSpec((bsz, d), lambda j: (0, 0)),
                  pl.BlockSpec((d, tn), lambda j: (0, j)),
                  pl.BlockSpec((1, tn), lambda j: (0, j))],
        out_specs=pl.BlockSpec((bsz, tn), lambda j: (0, j)),
        out_shape=jax.ShapeDtypeStruct((bsz, n), F32),
        name="adaln_mod",
    )(c, w_ada, b_ada.reshape(1, n))


def _h_kernel(x_ref, g_ref, sc_ref, sh_ref, o_ref):
    x = x_ref[...]
    y = x * lax.rsqrt(jnp.mean(x * x, axis=-1, keepdims=True) + EPS)
    y = y * g_ref[...]
    o_ref[...] = (y * (1.0 + sc_ref[0]) + sh_ref[0]).astype(o_ref.dtype)


def _modulated_norm(x2, norm_g, scale, shift, seq):
    m, d = x2.shape
    bsz = m // seq
    tm = 512
    nt = seq // tm
    return pl.pallas_call(
        _h_kernel,
        grid=(bsz, nt),
        in_specs=[pl.BlockSpec((tm, d), lambda b, i: (b * nt + i, 0)),
                  pl.BlockSpec((1, d), lambda b, i: (0, 0)),
                  pl.BlockSpec((1, 1, d), lambda b, i: (b, 0, 0)),
                  pl.BlockSpec((1, 1, d), lambda b, i: (b, 0, 0))],
        out_specs=pl.BlockSpec((tm, d), lambda b, i: (b * nt + i, 0)),
        out_shape=jax.ShapeDtypeStruct((m, d), BF16),
        name="modulated_norm",
    )(x2, norm_g.reshape(1, d), scale.reshape(bsz, 1, d), shift.reshape(bsz, 1, d))


def _proj_kernel(h_ref, w_ref, o_ref):
    o_ref[...] = _dot(h_ref[...], w_ref[...]).astype(o_ref.dtype)


def _project(h, w, out_dtype, tn, tm=1024):
    m, k = h.shape
    n = w.shape[1]
    return pl.pallas_call(
        _proj_kernel,
        grid=(m // tm, n // tn),
        in_specs=[pl.BlockSpec((tm, k), lambda i, j: (i, 0)),
                  pl.BlockSpec((k, tn), lambda i, j: (0, j))],
        out_specs=pl.BlockSpec((tm, tn), lambda i, j: (i, j)),
        out_shape=jax.ShapeDtypeStruct((m, n), out_dtype),
        compiler_params=pltpu.CompilerParams(vmem_limit_bytes=VMEM_LIMIT),
        name="in_proj",
    )(h, w)


def _unit_lower_inverses(a_list):
    n = a_list[0].shape[0]
    ri = lax.broadcasted_iota(jnp.int32, (n, n), 0)
    ci = lax.broadcasted_iota(jnp.int32, (n, n), 1)
    x = ri ^ ci
    eye = (ri == ci).astype(F32)
    ts = [eye - jnp.where(x == 1, a, 0.0) for a in a_list]
    s = 2
    while s < n:
        level = (x >= s) & (x < 2 * s)
        ps = [_mm3(jnp.where(level, a, 0.0), t) for a, t in zip(a_list, ts)]
        ts = [t - _mm3(t, p) for t, p in zip(ts, ps)]
        s *= 2
    return ts


def _gdn_kernel(q_ref, k_ref, v_ref, z_ref, misc_ref, cwq_ref, cwk_ref, cwv_ref,
                alog_ref, dtb_ref, ng_ref, o_ref, qs, ks, vs, gs, bs,
                us, ws, ats, qgs, kdt, edl):
    hd = pl.program_id(1)
    seq = q_ref.shape[0]
    c = LANES
    row = lax.broadcasted_iota(jnp.int32, (seq, c), 0)
    lane = lax.broadcasted_iota(jnp.int32, (seq, c), 1)

    def conv_silu(u_ref, w_ref):
        u = u_ref[...]
        w = w_ref[...]
        y = u * w[CONV_K - 1:CONV_K, :]
        for k in range(1, CONV_K):
            sh = jnp.where(row >= k, pltpu.roll(u, k, 0), 0.0)
            y = y + sh * w[CONV_K - 1 - k:CONV_K - k, :]
        return _silu(y)

    def l2n(x):
        return x * lax.rsqrt(jnp.sum(x * x, axis=-1, keepdims=True) + EPS)

    qs[...] = l2n(conv_silu(q_ref, cwq_ref)) * (GDN_DK ** -0.5)
    ks[...] = l2n(conv_silu(k_ref, cwk_ref))
    vs[...] = conv_silu(v_ref, cwv_ref)

    misc = misc_ref[...]
    aa = jnp.sum(jnp.where(lane == hd, misc, 0.0), axis=-1, keepdims=True)
    ba = jnp.sum(jnp.where(lane == hd + GDN_HEADS, misc, 0.0), axis=-1, keepdims=True)
    gs[...] = -jnp.exp(alog_ref[0]) * _softplus(aa + dtb_ref[0])
    bs[...] = jnp.broadcast_to(_sigmoid(ba), (seq, c))

    ri = lax.broadcasted_iota(jnp.int32, (c, c), 0)
    ci = lax.broadcasted_iota(jnp.int32, (c, c), 1)
    tri = (ri >= ci).astype(BF16)
    ng = ng_ref[...]

    def cumdecay(g):
        g1 = g.astype(BF16)
        r1 = g - g1.astype(F32)
        g2 = r1.astype(BF16)
        g3 = (r1 - g2.astype(F32)).astype(BF16)
        return _dot(tri, g1) + (_dot(tri, g2) + _dot(tri, g3))

    def prepare_group(gi, carry):
        ns = [gi * GDN_GROUP + j for j in range(GDN_GROUP)]
        sls = [pl.ds(pl.multiple_of(n * c, c), c) for n in ns]
        dcols = [cumdecay(gs[sl, :]) for sl in sls]
        ks_ = [ks[sl, :] for sl in sls]
        kbs = [k * bs[sl, :] for k, sl in zip(ks_, sls)]
        kq = [_dot_nt(jnp.concatenate([kb, qs[sl, :]], axis=0).astype(BF16), k.astype(BF16))
              for kb, k, sl in zip(kbs, ks_, sls)]
        gammas = [jnp.exp(jnp.where(ri >= ci, d - d.T, -jnp.inf)) for d in dcols]
        a_list = [jnp.where(ri > ci, m[:c] * gm, 0.0) for m, gm in zip(kq, gammas)]
        for sl, m, gm in zip(sls, kq, gammas):
            ats[sl, :] = (m[c:] * gm).astype(BF16)
        ecols = [jnp.exp(d) for d in dcols]
        rhs = [jnp.concatenate([vs[sl, :] * bs[sl, :], kb * e], axis=1)
               for sl, kb, e in zip(sls, kbs, ecols)]
        tinvs = _unit_lower_inverses(a_list)
        sols = [_mm3(t, r) for t, r in zip(tinvs, rhs)]
        for n, sl, sol, d, e, k in zip(ns, sls, sols, dcols, ecols, ks_):
            us[sl, :] = sol[:, :c]
            ws[sl, :] = sol[:, c:].astype(BF16)
            qgs[sl, :] = (qs[sl, :] * e).astype(BF16)
            dlast = d[c - 1:c, :]
            kdt[sl, :] = (k * jnp.exp(dlast - d)).T.astype(BF16)
            edl[n] = jnp.broadcast_to(jnp.exp(dlast), (8, c))
        return carry

    lax.fori_loop(0, seq // (c * GDN_GROUP), prepare_group, 0)

    def recur(n, state):
        sl = pl.ds(pl.multiple_of(n * c, c), c)
        sb = state.astype(BF16)
        v_new = (us[sl, :] - _dot(ws[sl, :], sb)).astype(BF16)
        o = _dot(qgs[sl, :], sb) + _dot(ats[sl, :], v_new)
        state = state * edl[n][0:1, :] + _dot(kdt[sl, :], v_new)
        on = o * lax.rsqrt(jnp.mean(o * o, axis=-1, keepdims=True) + EPS) * ng
        o_ref[sl, :] = (on * _silu(z_ref[sl, :])).astype(o_ref.dtype)
        return state

    lax.fori_loop(0, seq // c, recur, jnp.zeros((GDN_DK, GDN_DV), F32))


def _gdn(pa, pb, conv_w, a_log, dt_bias, gdn_norm_g, bsz, seq):
    nh = GDN_HEADS
    c = LANES
    alog_b = jnp.broadcast_to(a_log.reshape(nh, 1, 1), (nh, 1, c))
    dtb_b = jnp.broadcast_to(dt_bias.reshape(nh, 1, 1), (nh, 1, c))
    blk = lambda off: pl.BlockSpec((seq, c), lambda b, h: (b, off + h))
    cw = lambda off: pl.BlockSpec((CONV_K, c), lambda b, h: (0, off + h))
    hv = pl.BlockSpec((1, 1, c), lambda b, h: (h, 0, 0))
    return pl.pallas_call(
        _gdn_kernel,
        grid=(bsz, nh),
        in_specs=[blk(0), blk(nh), blk(2 * nh), blk(3 * nh),
                  pl.BlockSpec((seq, c), lambda b, h: (b, 2)),
                  cw(0), cw(nh), cw(2 * nh), hv, hv,
                  pl.BlockSpec((1, c), lambda b, h: (0, 0))],
        out_specs=pl.BlockSpec((seq, c), lambda b, h: (b, h)),
        out_shape=jax.ShapeDtypeStruct((bsz * seq, GDN_W), BF16),
        scratch_shapes=([pltpu.VMEM((seq, c), F32)] * 6
                        + [pltpu.VMEM((seq, c), BF16)] * 4
                        + [pltpu.VMEM((seq // c, 8, c), F32)]),
        compiler_params=pltpu.CompilerParams(vmem_limit_bytes=VMEM_LIMIT),
        name="gated_deltanet",
    )(pa, pa, pa, pa, pb, conv_w, conv_w, conv_w, alog_b, dtb_b,
      gdn_norm_g.reshape(1, c))


def _t5_bucket_table(n_tiles):
    q = np.arange(LANES)[:, None]
    s = np.arange(LANES)[None, :]
    out = []
    max_exact = NUM_BUCKETS // 2
    for d in range(n_tiles):
        n = np.maximum(d * LANES + q - s, 0)
        nf = np.maximum(n, 1).astype(np.float32)
        large = max_exact + (np.log(nf / max_exact) / math.log(MAX_DISTANCE / max_exact)
                             * (NUM_BUCKETS - max_exact)).astype(np.int32)
        large = np.minimum(large, NUM_BUCKETS - 1)
        out.append(np.where(n < max_exact, n, large))
    return np.stack(out).astype(np.int32)


_BIAS_TILES = 3
assert (_BIAS_TILES - 1) * LANES - (LANES - 1) >= MAX_DISTANCE


def _bias_kernel(bkt_ref, rel_ref, o_ref):
    hd = pl.program_id(0)
    for d in range(_BIAS_TILES):
        bkt = bkt_ref[d]
        acc = jnp.zeros((LANES, LANES), F32)
        for j in range(NUM_BUCKETS):
            acc = jnp.where(bkt == j, rel_ref[j, hd], acc)
        o_ref[d, 0] = acc


def _bias_tiles(rel_bias):
    bkt = jnp.asarray(_t5_bucket_table(_BIAS_TILES))
    return pl.pallas_call(
        _bias_kernel,
        grid=(DSA_HEADS,),
        in_specs=[pl.BlockSpec((_BIAS_TILES, LANES, LANES), lambda h: (0, 0, 0)),
                  pl.BlockSpec(memory_space=pltpu.SMEM)],
        out_specs=pl.BlockSpec((_BIAS_TILES, 1, LANES, LANES), lambda h: (0, h, 0, 0)),
        out_shape=jax.ShapeDtypeStruct((_BIAS_TILES, DSA_HEADS, LANES, LANES), F32),
        name="t5_bias_tiles",
    )(bkt, rel_bias)


def _dsa_kernel(qb_ref, qi_ref, misc_ref, zb_ref, kil_ref, kih_ref, ckv_ref,
                kvg_ref, wuk_ref, wuv_ref, bias_ref, o_ref,
                ckvn, wcb, keys, madd, qlat, lg, mx, ssum, pbuf, acc):
    i = pl.program_id(1)
    c = LANES
    sc_w = DSA_SUPER * c
    nck = i + 1
    nsc = i // DSA_SUPER + 1
    nh = DSA_HEADS

    @pl.when(i == 0)
    def _():
        x = ckv_ref[...]
        y = x * lax.rsqrt(jnp.mean(x * x, axis=-1, keepdims=True) + EPS)
        ckvn[...] = (y * kvg_ref[...]).astype(BF16)

    ri = lax.broadcasted_iota(jnp.int32, (c, c), 0)
    ci = lax.broadcasted_iota(jnp.int32, (c, c), 1)
    misc = misc_ref[...]
    wscale = IDX_HEADS ** -0.5 * IDX_DIM ** -0.5
    for h in range(IDX_HEADS):
        col = misc[:, 2 * GDN_HEADS + h:2 * GDN_HEADS + h + 1] * wscale
        wcb[h] = jnp.broadcast_to(col, (c, c))

    def score_super(s, carry):
        ccs = [s * DSA_SUPER + j for j in range(DSA_SUPER)]
        sls = [pl.ds(pl.multiple_of(cc * c, c), c) for cc in ccs]
        klhs = [jnp.concatenate([kil_ref[sl, :], kih_ref[sl, :]], axis=0) for sl in sls]
        dots = [[_dot_nt(qi_ref[:, p * c:(p + 1) * c], klh)
                 for p in range(IDX_HEADS // 2)] for klh in klhs]
        for cc, ds in zip(ccs, dots):
            acc_s = jnp.zeros((c, c), F32)
            for p, d in enumerate(ds):
                acc_s = acc_s + wcb[2 * p] * jnp.maximum(d[:, :c], 0.0)
                acc_s = acc_s + wcb[2 * p + 1] * jnp.maximum(d[:, c:], 0.0)
            adm = (cc * c + ci) <= (i * c + ri)
            acc_s = jnp.where(adm, acc_s, -jnp.inf)
            bits = lax.bitcast_convert_type(acc_s, jnp.int32)
            keys[cc] = jnp.where(bits < 0, bits ^ 0x7FFFFFFF, bits)
        return carry

    lax.fori_loop(0, nsc, score_super, 0)

    kf = float(TOPK_MAX)

    def count(pred):
        def body(s, cnt):
            for j in range(DSA_SUPER):
                cnt = cnt + jnp.where(pred(keys[s * DSA_SUPER + j]), 1.0, 0.0)
            return cnt
        cnt = lax.fori_loop(0, nsc, body, jnp.zeros((c, c), F32))
        return jnp.sum(cnt, axis=-1, keepdims=True)

    def bit_step(it, v):
        cand = v + jnp.left_shift(jnp.int32(1), 31 - it)
        tot = count(lambda kc: kc >= cand)
        return jnp.where(tot >= kf, cand, v)

    thr = lax.fori_loop(0, 32, bit_step, jnp.full((c, c), INT_MIN, jnp.int32))
    need = kf - count(lambda kc: kc > thr)
    upper = (ri < ci).astype(BF16)

    def mask_super(s, running):
        ccs = [s * DSA_SUPER + j for j in range(DSA_SUPER)]
        kcs = [keys[cc] for cc in ccs]
        eqs = [jnp.where(kc == thr, 1.0, 0.0) for kc in kcs]
        prefix = [_dot(eq.astype(BF16), upper) for eq in eqs]
        totals = [jnp.sum(eq, axis=-1, keepdims=True) for eq in eqs]
        for cc, kc, eq, pre, tot in zip(ccs, kcs, eqs, prefix, totals):
            sel = jnp.where(kc > thr, 1.0, jnp.where(running + pre < need, eq, 0.0))
            adm = (cc * c + ci) <= (i * c + ri)
            madd[cc] = jnp.where(adm & (sel > 0.5), 0.0, -jnp.inf)
            running = running + tot
        return running

    lax.fori_loop(0, nsc, mask_super, jnp.zeros((c, 1), F32))

    for h in range(nh):
        qh = qb_ref[:, h * DSA_DH:(h + 1) * DSA_DH]
        qlat[h * c:(h + 1) * c, :] = (
            _dot_nt(qh, wuk_ref[h]) * (DSA_DH ** -0.5)).astype(BF16)

    mx[...] = jnp.full(mx.shape, -jnp.inf, F32)
    ssum[...] = jnp.zeros(ssum.shape, F32)
    acc[...] = jnp.zeros(acc.shape, F32)

    def logit_super(s, carry):
        kv = ckvn[pl.ds(pl.multiple_of(s * sc_w, sc_w), sc_w), :]
        l_all = _dot_nt(qlat[...], kv)
        for j in range(DSA_SUPER):
            cc = s * DSA_SUPER + j
            ma = madd[cc]
            bidx = jnp.clip(i - cc, 0, _BIAS_TILES - 1)
            for h in range(nh):
                l = l_all[h * c:(h + 1) * c, j * c:(j + 1) * c] + bias_ref[bidx, h] + ma
                lg[s, h * c:(h + 1) * c, j * c:(j + 1) * c] = l
                mx[h * c:(h + 1) * c, :] = jnp.maximum(mx[h * c:(h + 1) * c, :], l)
        return carry

    lax.fori_loop(0, nsc, logit_super, 0)

    for h in range(nh):
        m = jnp.max(mx[h * c:(h + 1) * c, :], axis=-1, keepdims=True)
        mx[h * c:(h + 1) * c, :] = jnp.broadcast_to(m, (c, c))

    def pv_super(s, carry):
        for j in range(DSA_SUPER):
            for h in range(nh):
                rows = slice(h * c, (h + 1) * c)
                p = jnp.exp(lg[s, rows, j * c:(j + 1) * c] - mx[rows, :])
                ssum[rows, :] = ssum[rows, :] + p
                pbuf[rows, j * c:(j + 1) * c] = p.astype(BF16)
        kv = ckvn[pl.ds(pl.multiple_of(s * sc_w, sc_w), sc_w), :]
        acc[...] = acc[...] + _dot(pbuf[...], kv)
        return carry

    lax.fori_loop(0, nsc, pv_super, 0)

    for h in range(nh):
        rows = slice(h * c, (h + 1) * c)
        denom = jnp.sum(ssum[rows, :], axis=-1, keepdims=True)
        olat = (acc[rows, :] / denom).astype(BF16)
        ob = _dot(olat, wuv_ref[h])
        zb = zb_ref[:, h * DSA_DH:(h + 1) * DSA_DH]
        o_ref[:, h * DSA_DH:(h + 1) * DSA_DH] = (ob * _silu(zb)).astype(o_ref.dtype)


def _dsa(pa, pb, pc, kv_norm_g, wuk_t, wuv_t, bias_tiles, bsz, seq):
    c = LANES
    nb = seq // c
    nsc = nb // DSA_SUPER
    rows_all = DSA_HEADS * c
    za_off = (4 * GDN_W + 2 * D_MODEL) // DSA_W
    qi_off = DSA_W // (IDX_HEADS * IDX_DIM)
    kil_off = (DSA_W + IDX_HEADS * IDX_DIM) // c
    row = lambda b, i: b * nb + i
    return pl.pallas_call(
        _dsa_kernel,
        grid=(bsz, nb),
        in_specs=[
            pl.BlockSpec((c, DSA_W), lambda b, i: (row(b, i), 0)),
            pl.BlockSpec((c, IDX_HEADS * IDX_DIM), lambda b, i: (row(b, i), qi_off)),
            pl.BlockSpec((c, c), lambda b, i: (row(b, i), 2)),
            pl.BlockSpec((c, DSA_W), lambda b, i: (row(b, i), za_off)),
            pl.BlockSpec((seq, c), lambda b, i: (b, kil_off)),
            pl.BlockSpec((seq, c), lambda b, i: (b, kil_off + 1)),
            pl.BlockSpec((seq, KV_RANK), lambda b, i: (b, 0)),
            pl.BlockSpec((1, KV_RANK), lambda b, i: (0, 0)),
            pl.BlockSpec((DSA_HEADS, KV_RANK, DSA_DH), lambda b, i: (0, 0, 0)),
            pl.BlockSpec((DSA_HEADS, KV_RANK, DSA_DH), lambda b, i: (0, 0, 0)),
            pl.BlockSpec((_BIAS_TILES, DSA_HEADS, c, c), lambda b, i: (0, 0, 0, 0)),
        ],
        out_specs=pl.BlockSpec((c, DSA_W), lambda b, i: (row(b, i), 0)),
        out_shape=jax.ShapeDtypeStruct((bsz * seq, DSA_W), BF16),
        scratch_shapes=[pltpu.VMEM((seq, KV_RANK), BF16),
                        pltpu.VMEM((IDX_HEADS, c, c), F32),
                        pltpu.VMEM((nb, c, c), jnp.int32),
                        pltpu.VMEM((nb, c, c), F32),
                        pltpu.VMEM((rows_all, KV_RANK), BF16),
                        pltpu.VMEM((nsc, rows_all, DSA_SUPER * c), F32),
                        pltpu.VMEM((rows_all, c), F32),
                        pltpu.VMEM((rows_all, c), F32),
                        pltpu.VMEM((rows_all, DSA_SUPER * c), BF16),
                        pltpu.VMEM((rows_all, KV_RANK), F32)],
        compiler_params=pltpu.CompilerParams(vmem_limit_bytes=VMEM_LIMIT),
        name="dsa_attention",
    )(pc, pc, pb, pa, pc, pc, pb, kv_norm_g.reshape(1, KV_RANK), wuk_t, wuv_t,
      bias_tiles)


def _out_kernel(oa_ref, ob_ref, gl_ref, x_ref, gate_ref, woa_ref, wob_ref, wout_ref,
                fg_ref, o_ref):
    d = x_ref.shape[1]
    ya = _dot(oa_ref[...], woa_ref[...])
    yb = _dot(ob_ref[...], wob_ref[...])
    gl = gl_ref[...]
    mix = _sigmoid(gl[:, :d]) * ya + _sigmoid(gl[:, d:]) * yb
    out = _dot(mix.astype(BF16), wout_ref[...])
    xn = x_ref[...] + gate_ref[0] * out
    y = xn * lax.rsqrt(jnp.mean(xn * xn, axis=-1, keepdims=True) + EPS)
    o_ref[...] = y * fg_ref[...]


def _merge_out(oa, ob, pa, x2, gate, w_o_gdn, w_o_dsa, w_out, final_g, bsz, seq):
    d = D_MODEL
    tm = 512
    nt = seq // tm
    gl_off = (4 * GDN_W) // (2 * d)
    row = lambda b, i: (b * nt + i, 0)
    full = lambda b, i: (0, 0)
    return pl.pallas_call(
        _out_kernel,
        grid=(bsz, nt),
        in_specs=[pl.BlockSpec((tm, GDN_W), row),
                  pl.BlockSpec((tm, DSA_W), row),
                  pl.BlockSpec((tm, 2 * d), lambda b, i: (b * nt + i, gl_off)),
                  pl.BlockSpec((tm, d), row),
                  pl.BlockSpec((1, 1, d), lambda b, i: (b, 0, 0)),
                  pl.BlockSpec((GDN_W, d), full),
                  pl.BlockSpec((DSA_W, d), full),
                  pl.BlockSpec((d, d), full),
                  pl.BlockSpec((1, d), full)],
        out_specs=pl.BlockSpec((tm, d), row),
        out_shape=jax.ShapeDtypeStruct((bsz * seq, d), F32),
        compiler_params=pltpu.CompilerParams(vmem_limit_bytes=VMEM_LIMIT),
        name="merge_out",
    )(oa, ob, pa, x2, gate.reshape(bsz, 1, d), w_o_gdn, w_o_dsa, w_out,
      final_g.reshape(1, d))


def _split_w_in(w):
    o = 0
    names = ("qa", "ka", "va", "za", "aa", "ba", "qb", "ckv", "zb", "qi", "ki", "wi", "gl")
    sizes = (GDN_HEADS * GDN_DK, GDN_HEADS * GDN_DK, GDN_W, GDN_W, GDN_HEADS, GDN_HEADS,
             DSA_W, KV_RANK, DSA_W, IDX_HEADS * IDX_DIM, IDX_DIM, IDX_HEADS, 2 * D_MODEL)
    col = {}
    for nm, sz in zip(names, sizes):
        col[nm] = w[:, o:o + sz]
        o += sz
    k = w.shape[0]
    z = lambda n: jnp.zeros((k, n), w.dtype)
    wa = jnp.concatenate([col["qa"], col["ka"], col["va"], col["za"], col["gl"], col["zb"]], 1)
    n_misc = 2 * GDN_HEADS + IDX_HEADS
    wb = jnp.concatenate([col["ckv"], col["aa"], col["ba"], col["wi"], z(LANES - n_misc)], 1)
    pad = z(LANES - IDX_DIM)
    wc = jnp.concatenate([col["qb"], col["qi"], col["ki"], pad, pad, col["ki"]], 1)
    return wa.astype(BF16), wb.astype(BF16), wc.astype(BF16)


def kernel(x, c, norm_g, w_ada, b_ada, w_in, conv_w, a_log, dt_bias, gdn_norm_g, w_o_gdn,
           kv_norm_g, w_uk, w_uv, w_o_dsa, w_out, rel_bias, final_g):
    bsz, seq, d = x.shape
    depth = norm_g.shape[0]
    assert depth == 1, "the final RMSNorm is fused into the layer's merge kernel"
    x2 = x.reshape(bsz * seq, d)
    bias_tiles = _bias_tiles(rel_bias)
    for l in range(depth):
        mod = _adaln_mod(c, w_ada[l], b_ada[l])
        shift, scale, gate = mod[:, :d], mod[:, d:2 * d], mod[:, 2 * d:]
        h = _modulated_norm(x2, norm_g[l], scale, shift, seq)
        wa, wb, wc = _split_w_in(w_in[l])
        pa = _project(h, wa, F32, 512)
        pb = _project(h, wb, F32, wb.shape[1])
        pc = _project(h, wc, BF16, wc.shape[1] // 2)
        oa = _gdn(pa, pb, conv_w[l], a_log[l], dt_bias[l], gdn_norm_g[l], bsz, seq)
        wuk_t = jnp.transpose(w_uk[l], (1, 0, 2)).astype(BF16)
        wuv_t = jnp.transpose(w_uv[l], (1, 0, 2)).astype(BF16)
        ob = _dsa(pa, pb, pc, kv_norm_g[l], wuk_t, wuv_t, bias_tiles, bsz, seq)
        x2 = _merge_out(oa, ob, pa, x2, gate, w_o_gdn[l].astype(BF16),
                        w_o_dsa[l].astype(BF16), w_out[l].astype(BF16), final_g,
                        bsz, seq)
    return x2.reshape(bsz, seq, d)
```

```python
import math

import numpy as np
import jax
import jax.numpy as jnp
from jax import lax
from jax.experimental import pallas as pl
from jax.experimental.pallas import tpu as pltpu

D_MODEL = 1024
GDN_HEADS = 8
GDN_DK = 128
GDN_DV = 128
CONV_K = 4
DSA_HEADS = 8
DSA_DH = 128
KV_RANK = 256
IDX_HEADS = 8
IDX_DIM = 64
TOPK_MAX = 256
NUM_BUCKETS = 32
MAX_DISTANCE = 128
EPS = 1e-6

LANES = 128
SUBLANES = 8
GDN_GROUP = 8
DSA_SUPER = 4
GDN_W = GDN_HEADS * GDN_DV
DSA_W = DSA_HEADS * DSA_DH
LOG2E = math.log2(math.e)

F32 = jnp.float32
BF16 = jnp.bfloat16
INT_MIN = -2 ** 31
VMEM_LIMIT = 48 * 1024 * 1024

_NT = (((1,), (1,)), ((), ()))


def _dot(a, b):
    return jnp.dot(a, b, preferred_element_type=F32)


def _dot_nt(a, b):
    return lax.dot_general(a, b, _NT, preferred_element_type=F32)


def _split2(a):
    hi = a.astype(BF16)
    lo = (a - hi.astype(F32)).astype(BF16)
    return hi, lo


def _mm3(a, b):
    ah, al = _split2(a)
    bh, bl = _split2(b)
    n = b.shape[1]
    big = _dot(ah, jnp.concatenate([bh, bl], axis=1))
    return big[:, :n] + (big[:, n:] + _dot(al, bh))


def _sigmoid(x):
    return 1.0 / (1.0 + jnp.exp(-x))


def _silu(x):
    return x * _sigmoid(x)


def _softplus(x):
    return jnp.maximum(x, 0.0) + jnp.log1p(jnp.exp(-jnp.abs(x)))


def _fold_rows(x, op):
    parts = [x[r:r + SUBLANES, :] for r in range(0, x.shape[0], SUBLANES)]
    while len(parts) > 1:
        nxt = [op(parts[j], parts[j + 1]) for j in range(0, len(parts) - 1, 2)]
        if len(parts) % 2:
            nxt.append(parts[-1])
        parts = nxt
    return parts[0]


def _mod_kernel(c_ref, w_ref, b_ref, o_ref):
    a = _silu(c_ref[...])
    o_ref[...] = _mm3(a, w_ref[...]) + b_ref[...]


def _adaln_mod(c, w_ada, b_ada):
    bsz, d = c.shape
    n = w_ada.shape[1]
    tn = 512
    return pl.pallas_call(
        _mod_kernel,
        grid=(n // tn,),
        in_specs=[pl.BlockSpec((bsz, d), lambda j: (0, 0)),
                  pl.BlockSpec((d, tn), lambda j: (0, j)),
                  pl.BlockSpec((1, tn), lambda j: (0, j))],
        out_specs=pl.BlockSpec((bsz, tn), lambda j: (0, j)),
        out_shape=jax.ShapeDtypeStruct((bsz, n), F32),
        name="adaln_mod",
    )(c, w_ada, b_ada.reshape(1, n))


def _h_kernel(x_ref, g_ref, sc_ref, sh_ref, o_ref):
    x = x_ref[...]
    y = x * lax.rsqrt(jnp.mean(x * x, axis=-1, keepdims=True) + EPS)
    y = y * g_ref[...]
    o_ref[...] = (y * (1.0 + sc_ref[0]) + sh_ref[0]).astype(o_ref.dtype)


def _modulated_norm(x2, norm_g, scale, shift, seq):
    m, d = x2.shape
    bsz = m // seq
    tm = 512
    nt = seq // tm
    return pl.pallas_call(
        _h_kernel,
        grid=(bsz, nt),
        in_specs=[pl.BlockSpec((tm, d), lambda b, i: (b * nt + i, 0)),
                  pl.BlockSpec((1, d), lambda b, i: (0, 0)),
                  pl.BlockSpec((1, 1, d), lambda b, i: (b, 0, 0)),
                  pl.BlockSpec((1, 1, d), lambda b, i: (b, 0, 0))],
        out_specs=pl.BlockSpec((tm, d), lambda b, i: (b * nt + i, 0)),
        out_shape=jax.ShapeDtypeStruct((m, d), BF16),
        name="modulated_norm",
    )(x2, norm_g.reshape(1, d), scale.reshape(bsz, 1, d), shift.reshape(bsz, 1, d))


def _proj_kernel(h_ref, w_ref, o_ref):
    o_ref[...] = _dot(h_ref[...], w_ref[...]).astype(o_ref.dtype)


def _project(h, w, out_dtype, tn, tm=1024):
    m, k = h.shape
    n = w.shape[1]
    return pl.pallas_call(
        _proj_kernel,
        grid=(m // tm, n // tn),
        in_specs=[pl.BlockSpec((tm, k), lambda i, j: (i, 0)),
                  pl.BlockSpec((k, tn), lambda i, j: (0, j))],
        out_specs=pl.BlockSpec((tm, tn), lambda i, j: (i, j)),
        out_shape=jax.ShapeDtypeStruct((m, n), out_dtype),
        compiler_params=pltpu.CompilerParams(vmem_limit_bytes=VMEM_LIMIT),
        name="in_proj",
    )(h, w)


def _unit_lower_inverses(a_list):
    n = a_list[0].shape[0]
    ri = lax.broadcasted_iota(jnp.int32, (n, n), 0)
    ci = lax.broadcasted_iota(jnp.int32, (n, n), 1)
    x = ri ^ ci
    eye = (ri == ci).astype(F32)
    ts = [eye - jnp.where(x == 1, a, 0.0) for a in a_list]
    s = 2
    while s < n:
        level = (x >= s) & (x < 2 * s)
        tb = [t.astype(BF16) for t in ts]
        ps = [_dot(jnp.where(level, a, 0.0).astype(BF16), t) for a, t in zip(a_list, tb)]
        ts = [t - _dot(t16, p.astype(BF16)) for t, t16, p in zip(ts, tb, ps)]
        s *= 2
    return ts


def _refined_solves(t_list, a_list, rhs_list):
    x0 = [_mm3(t, r) for t, r in zip(t_list, rhs_list)]
    res = [r - (x + _mm3(a, x)) for r, x, a in zip(rhs_list, x0, a_list)]
    return [x + _dot(t.astype(BF16), e.astype(BF16)) for x, t, e in zip(x0, t_list, res)]


def _gdn_kernel(q_ref, k_ref, v_ref, z_ref, misc_ref, cwq_ref, cwk_ref, cwv_ref,
                alog_ref, dtb_ref, ng_ref, o_ref, qs, ks, vs, gs, bs,
                ous, cus, qws, nws, edl):
    hd = pl.program_id(1)
    seq = q_ref.shape[0]
    c = LANES
    row = lax.broadcasted_iota(jnp.int32, (seq, c), 0)
    lane = lax.broadcasted_iota(jnp.int32, (seq, c), 1)

    def conv_silu(u_ref, w_ref):
        u = u_ref[...]
        w = w_ref[...]
        y = u * w[CONV_K - 1:CONV_K, :]
        for k in range(1, CONV_K):
            sh = jnp.where(row >= k, pltpu.roll(u, k, 0), 0.0)
            y = y + sh * w[CONV_K - 1 - k:CONV_K - k, :]
        return _silu(y)

    def l2n(x):
        return x * lax.rsqrt(jnp.sum(x * x, axis=-1, keepdims=True) + EPS)

    qs[...] = l2n(conv_silu(q_ref, cwq_ref)) * (GDN_DK ** -0.5)
    ks[...] = l2n(conv_silu(k_ref, cwk_ref))
    vs[...] = conv_silu(v_ref, cwv_ref)

    misc = misc_ref[...]
    aa = jnp.sum(jnp.where(lane == hd, misc, 0.0), axis=-1, keepdims=True)
    ba = jnp.sum(jnp.where(lane == hd + GDN_HEADS, misc, 0.0), axis=-1, keepdims=True)
    gs[...] = -jnp.exp(alog_ref[0]) * _softplus(aa + dtb_ref[0])
    bs[...] = jnp.broadcast_to(_sigmoid(ba), (seq, c))

    ri = lax.broadcasted_iota(jnp.int32, (c, c), 0)
    ci = lax.broadcasted_iota(jnp.int32, (c, c), 1)
    tri = (ri >= ci).astype(BF16)
    ng = ng_ref[...]

    def cumdecay(g):
        g1 = g.astype(BF16)
        r1 = g - g1.astype(F32)
        g2 = r1.astype(BF16)
        g3 = (r1 - g2.astype(F32)).astype(BF16)
        d = _dot(tri, jnp.concatenate([g1, g2, g3], axis=1))
        return d[:, :c] + (d[:, c:2 * c] + d[:, 2 * c:])

    def prepare_group(gi, carry):
        ns = [gi * GDN_GROUP + j for j in range(GDN_GROUP)]
        sls = [pl.ds(pl.multiple_of(n * c, c), c) for n in ns]
        dcols = [cumdecay(gs[sl, :]) for sl in sls]
        ks_ = [ks[sl, :] for sl in sls]
        kbs = [k * bs[sl, :] for k, sl in zip(ks_, sls)]
        kq = [_dot_nt(jnp.concatenate([kb, qs[sl, :]], axis=0).astype(BF16), k.astype(BF16))
              for kb, k, sl in zip(kbs, ks_, sls)]
        gammas = [jnp.exp(jnp.where(ri >= ci, d - d.T, -jnp.inf)) for d in dcols]
        a_list = [jnp.where(ri > ci, m[:c] * gm, 0.0) for m, gm in zip(kq, gammas)]
        attns = [(m[c:] * gm).astype(BF16) for m, gm in zip(kq, gammas)]
        ecols = [jnp.exp(d) for d in dcols]
        rhs = [jnp.concatenate([kb * e, vs[sl, :] * bs[sl, :]], axis=1)
               for sl, kb, e in zip(sls, kbs, ecols)]
        tinvs = _unit_lower_inverses(a_list)
        wu = [x.astype(BF16) for x in _refined_solves(tinvs, a_list, rhs)]
        kdts = [(k * jnp.exp(d[c - 1:c, :] - d)).T.astype(BF16) for k, d in zip(ks_, dcols)]
        aw = [_dot(at, x) for at, x in zip(attns, wu)]
        kw = [_dot(kd, x) for kd, x in zip(kdts, wu)]
        for n, sl, d, e, a_, k_ in zip(ns, sls, dcols, ecols, aw, kw):
            qws[sl, :] = (qs[sl, :] * e - a_[:, :c]).astype(BF16)
            ous[sl, :] = a_[:, c:]
            nws[sl, :] = k_[:, :c].astype(BF16)
            cus[sl, :] = k_[:, c:]
            edl[n] = jnp.broadcast_to(jnp.exp(d[c - 1:c, :]), (SUBLANES, c))
        return carry

    lax.fori_loop(0, seq // (c * GDN_GROUP), prepare_group, 0)

    def recur(n, state):
        sl = pl.ds(pl.multiple_of(n * c, c), c)
        sb = state.astype(BF16)
        ous[sl, :] = _dot(qws[sl, :], sb) + ous[sl, :]
        return state * edl[n][0:1, :] + (cus[sl, :] - _dot(nws[sl, :], sb))

    lax.fori_loop(0, seq // c, recur, jnp.zeros((GDN_DK, GDN_DV), F32))

    o = ous[...]
    on = o * lax.rsqrt(jnp.mean(o * o, axis=-1, keepdims=True) + EPS) * ng
    o_ref[...] = (on * _silu(z_ref[...])).astype(o_ref.dtype)


def _gdn(pa, pb, conv_w, a_log, dt_bias, gdn_norm_g, bsz, seq):
    nh = GDN_HEADS
    c = LANES
    alog_b = jnp.broadcast_to(a_log.reshape(nh, 1, 1), (nh, 1, c))
    dtb_b = jnp.broadcast_to(dt_bias.reshape(nh, 1, 1), (nh, 1, c))
    blk = lambda off: pl.BlockSpec((seq, c), lambda b, h: (b, off + h))
    cw = lambda off: pl.BlockSpec((CONV_K, c), lambda b, h: (0, off + h))
    hv = pl.BlockSpec((1, 1, c), lambda b, h: (h, 0, 0))
    return pl.pallas_call(
        _gdn_kernel,
        grid=(bsz, nh),
        in_specs=[blk(0), blk(nh), blk(2 * nh), blk(3 * nh),
                  pl.BlockSpec((seq, c), lambda b, h: (b, 2)),
                  cw(0), cw(nh), cw(2 * nh), hv, hv,
                  pl.BlockSpec((1, c), lambda b, h: (0, 0))],
        out_specs=pl.BlockSpec((seq, c), lambda b, h: (b, h)),
        out_shape=jax.ShapeDtypeStruct((bsz * seq, GDN_W), BF16),
        scratch_shapes=([pltpu.VMEM((seq, c), F32)] * 7
                        + [pltpu.VMEM((seq, c), BF16)] * 2
                        + [pltpu.VMEM((seq // c, SUBLANES, c), F32)]),
        compiler_params=pltpu.CompilerParams(vmem_limit_bytes=VMEM_LIMIT),
        name="gated_deltanet",
    )(pa, pa, pa, pa, pb, conv_w, conv_w, conv_w, alog_b, dtb_b,
      gdn_norm_g.reshape(1, c))


def _t5_bucket_table(n_tiles):
    s = np.arange(LANES)[:, None]
    q = np.arange(LANES)[None, :]
    out = []
    max_exact = NUM_BUCKETS // 2
    for d in range(n_tiles):
        n = np.maximum(d * LANES + q - s, 0)
        nf = np.maximum(n, 1).astype(np.float32)
        large = max_exact + (np.log(nf / max_exact) / math.log(MAX_DISTANCE / max_exact)
                             * (NUM_BUCKETS - max_exact)).astype(np.int32)
        large = np.minimum(large, NUM_BUCKETS - 1)
        out.append(np.where(n < max_exact, n, large))
    return np.stack(out).astype(np.int32)


_BIAS_TILES = 3
assert (_BIAS_TILES - 1) * LANES - (LANES - 1) >= MAX_DISTANCE


def _bias_kernel(bkt_ref, rel_ref, o_ref):
    hd = pl.program_id(0)
    for d in range(_BIAS_TILES):
        bkt = bkt_ref[d]
        acc = jnp.zeros((LANES, LANES), F32)
        for j in range(NUM_BUCKETS):
            acc = jnp.where(bkt == j, rel_ref[j, hd] * LOG2E, acc)
        o_ref[d, 0] = acc


def _bias_tiles(rel_bias):
    bkt = jnp.asarray(_t5_bucket_table(_BIAS_TILES))
    return pl.pallas_call(
        _bias_kernel,
        grid=(DSA_HEADS,),
        in_specs=[pl.BlockSpec((_BIAS_TILES, LANES, LANES), lambda h: (0, 0, 0)),
                  pl.BlockSpec(memory_space=pltpu.SMEM)],
        out_specs=pl.BlockSpec((_BIAS_TILES, 1, LANES, LANES), lambda h: (0, h, 0, 0)),
        out_shape=jax.ShapeDtypeStruct((_BIAS_TILES, DSA_HEADS, LANES, LANES), F32),
        name="t5_bias_tiles",
    )(bkt, rel_bias)


def _dsa_kernel(qb_ref, qi_ref, misc_ref, zb_ref, kil_ref, kih_ref, ckv_ref,
                kvg_ref, wuk_ref, wuvt_ref, bias_ref, o_ref,
                ckvn, ckvt, keys, madd, qlat, lg, pbuf, acc):
    i = pl.program_id(1)
    c = LANES
    sc_w = DSA_SUPER * c
    nsc = i // DSA_SUPER + 1
    nh = DSA_HEADS
    seq = ckv_ref.shape[0]

    @pl.when(i == 0)
    def _():
        x = ckv_ref[...]
        y = x * lax.rsqrt(jnp.mean(x * x, axis=-1, keepdims=True) + EPS) * kvg_ref[...]
        ckvn[...] = y.astype(BF16)
        for r in range(seq // c):
            ckvt[r // DSA_SUPER, :, (r % DSA_SUPER) * c:(r % DSA_SUPER + 1) * c] = (
                y[r * c:(r + 1) * c, :].T.astype(BF16))

    ri = lax.broadcasted_iota(jnp.int32, (c, c), 0)
    ci = lax.broadcasted_iota(jnp.int32, (c, c), 1)
    misc_t = misc_ref[...].T
    wscale = IDX_HEADS ** -0.5 * IDX_DIM ** -0.5
    wrow = [misc_t[2 * GDN_HEADS + h:2 * GDN_HEADS + h + 1, :] * wscale
            for h in range(IDX_HEADS)]
    npair = IDX_HEADS // 2
    qi4 = jnp.concatenate([qi_ref[:, p * c:(p + 1) * c] for p in range(npair)], axis=0)

    def score_super(s, carry):
        ccs = [s * DSA_SUPER + j for j in range(DSA_SUPER)]
        sls = [pl.ds(pl.multiple_of(cc * c, c), c) for cc in ccs]
        dots = [_dot_nt(jnp.concatenate([kil_ref[sl, :], kih_ref[sl, :]], axis=0), qi4)
                for sl in sls]
        for cc, d in zip(ccs, dots):
            acc_s = jnp.zeros((c, c), F32)
            for p in range(npair):
                acc_s = acc_s + wrow[2 * p] * jnp.maximum(d[:c, p * c:(p + 1) * c], 0.0)
                acc_s = acc_s + wrow[2 * p + 1] * jnp.maximum(d[c:, p * c:(p + 1) * c], 0.0)
            adm = (cc * c + ri) <= (i * c + ci)
            acc_s = jnp.where(adm, acc_s, -jnp.inf)
            bits = lax.bitcast_convert_type(acc_s, jnp.int32)
            keys[cc] = jnp.where(bits < 0, bits ^ 0x7FFFFFFF, bits)
        return carry

    lax.fori_loop(0, nsc, score_super, 0)

    kf = float(TOPK_MAX)

    def count(pred):
        def body(s, cnt):
            parts = [_fold_rows(jnp.where(pred(keys[s * DSA_SUPER + j]), 1.0, 0.0), jnp.add)
                     for j in range(DSA_SUPER)]
            return cnt + ((parts[0] + parts[1]) + (parts[2] + parts[3]))
        cnt = lax.fori_loop(0, nsc, body, jnp.zeros((SUBLANES, c), F32))
        return jnp.sum(cnt, axis=0, keepdims=True)

    def bit_step(it, v):
        cand = v + jnp.left_shift(jnp.int32(1), 31 - it)
        tot = count(lambda kc: kc >= cand)
        return jnp.where(tot >= kf, cand, v)

    thr = lax.fori_loop(0, 32, bit_step, jnp.full((1, c), INT_MIN, jnp.int32))
    need = kf - count(lambda kc: kc > thr)
    lower = (ci < ri).astype(BF16)

    def mask_super(s, running):
        ccs = [s * DSA_SUPER + j for j in range(DSA_SUPER)]
        kcs = [keys[cc] for cc in ccs]
        eqs = [jnp.where(kc == thr, 1.0, 0.0) for kc in kcs]
        prefix = [_dot(lower, eq.astype(BF16)) for eq in eqs]
        totals = [jnp.sum(_fold_rows(eq, jnp.add), axis=0, keepdims=True) for eq in eqs]
        for cc, kc, eq, pre, tot in zip(ccs, kcs, eqs, prefix, totals):
            sel = jnp.where(kc > thr, 1.0, jnp.where(running + pre < need, eq, 0.0))
            adm = (cc * c + ri) <= (i * c + ci)
            madd[cc] = jnp.where(adm & (sel > 0.5), 0.0, -jnp.inf)
            running = running + tot
        return running

    lax.fori_loop(0, nsc, mask_super, jnp.zeros((1, c), F32))

    for h in range(nh):
        qh = qb_ref[:, h * DSA_DH:(h + 1) * DSA_DH]
        qlat[h * c:(h + 1) * c, :] = (
            _dot_nt(qh, wuk_ref[h]) * (DSA_DH ** -0.5 * LOG2E)).astype(BF16)

    acc[...] = jnp.zeros(acc.shape, F32)

    def logit_super(s, mx):
        kv = ckvn[pl.ds(pl.multiple_of(s * sc_w, sc_w), sc_w), :]
        l_all = _dot_nt(kv, qlat[...])
        new = []
        for h in range(nh):
            mh = mx[:, h * c:(h + 1) * c]
            for j in range(DSA_SUPER):
                cc = s * DSA_SUPER + j
                bidx = jnp.clip(i - cc, 0, _BIAS_TILES - 1)
                l = (l_all[j * c:(j + 1) * c, h * c:(h + 1) * c]
                     + bias_ref[bidx, h] + madd[cc])
                lg[s, j * c:(j + 1) * c, h * c:(h + 1) * c] = l
                mh = jnp.maximum(mh, _fold_rows(l, jnp.maximum))
            new.append(mh)
        return jnp.concatenate(new, axis=1)

    mx = lax.fori_loop(0, nsc, logit_super,
                       jnp.full((SUBLANES, nh * c), -jnp.inf, F32))
    m = jnp.max(mx, axis=0, keepdims=True)

    def pv_super(s, ssum):
        new = []
        for h in range(nh):
            sh = ssum[:, h * c:(h + 1) * c]
            mh = m[:, h * c:(h + 1) * c]
            for j in range(DSA_SUPER):
                p = jnp.exp2(lg[s, j * c:(j + 1) * c, h * c:(h + 1) * c] - mh)
                sh = sh + _fold_rows(p, jnp.add)
                pbuf[j * c:(j + 1) * c, h * c:(h + 1) * c] = p.astype(BF16)
            new.append(sh)
        acc[...] = acc[...] + _dot(ckvt[s], pbuf[...])
        return jnp.concatenate(new, axis=1)

    ssum = lax.fori_loop(0, nsc, pv_super, jnp.zeros((SUBLANES, nh * c), F32))
    denom = jnp.sum(ssum, axis=0, keepdims=True)

    for h in range(nh):
        cols = slice(h * c, (h + 1) * c)
        olat_t = (acc[:, cols] / denom[:, cols]).astype(BF16)
        ob = _dot(wuvt_ref[h], olat_t).T
        zb = zb_ref[:, h * DSA_DH:(h + 1) * DSA_DH]
        o_ref[:, h * DSA_DH:(h + 1) * DSA_DH] = (ob * _silu(zb)).astype(o_ref.dtype)


def _dsa(pa, pb, pc, kv_norm_g, wuk_t, wuv_t, bias_tiles, bsz, seq):
    c = LANES
    nb = seq // c
    nsc = nb // DSA_SUPER
    rows_all = DSA_HEADS * c
    za_off = (4 * GDN_W + 2 * D_MODEL) // DSA_W
    qi_off = DSA_W // (IDX_HEADS * IDX_DIM)
    kil_off = (DSA_W + IDX_HEADS * IDX_DIM) // c
    row = lambda b, i: b * nb + i
    return pl.pallas_call(
        _dsa_kernel,
        grid=(bsz, nb),
        in_specs=[
            pl.BlockSpec((c, DSA_W), lambda b, i: (row(b, i), 0)),
            pl.BlockSpec((c, IDX_HEADS * IDX_DIM), lambda b, i: (row(b, i), qi_off)),
            pl.BlockSpec((c, c), lambda b, i: (row(b, i), 2)),
            pl.BlockSpec((c, DSA_W), lambda b, i: (row(b, i), za_off)),
            pl.BlockSpec((seq, c), lambda b, i: (b, kil_off)),
            pl.BlockSpec((seq, c), lambda b, i: (b, kil_off + 1)),
            pl.BlockSpec((seq, KV_RANK), lambda b, i: (b, 0)),
            pl.BlockSpec((1, KV_RANK), lambda b, i: (0, 0)),
            pl.BlockSpec((DSA_HEADS, KV_RANK, DSA_DH), lambda b, i: (0, 0, 0)),
            pl.BlockSpec((DSA_HEADS, DSA_DH, KV_RANK), lambda b, i: (0, 0, 0)),
            pl.BlockSpec((_BIAS_TILES, DSA_HEADS, c, c), lambda b, i: (0, 0, 0, 0)),
        ],
        out_specs=pl.BlockSpec((c, DSA_W), lambda b, i: (row(b, i), 0)),
        out_shape=jax.ShapeDtypeStruct((bsz * seq, DSA_W), BF16),
        scratch_shapes=[pltpu.VMEM((seq, KV_RANK), BF16),
                        pltpu.VMEM((nsc, KV_RANK, DSA_SUPER * c), BF16),
                        pltpu.VMEM((nb, c, c), jnp.int32),
                        pltpu.VMEM((nb, c, c), F32),
                        pltpu.VMEM((rows_all, KV_RANK), BF16),
                        pltpu.VMEM((nsc, DSA_SUPER * c, rows_all), F32),
                        pltpu.VMEM((DSA_SUPER * c, rows_all), BF16),
                        pltpu.VMEM((KV_RANK, rows_all), F32)],
        compiler_params=pltpu.CompilerParams(vmem_limit_bytes=VMEM_LIMIT),
        name="dsa_attention",
    )(pc, pc, pb, pa, pc, pc, pb, kv_norm_g.reshape(1, KV_RANK), wuk_t, wuv_t,
      bias_tiles)


def _out_kernel(oa_ref, ob_ref, gl_ref, x_ref, gate_ref, woa_ref, wob_ref, wout_ref,
                fg_ref, o_ref):
    d = x_ref.shape[1]
    ya = _dot(oa_ref[...], woa_ref[...])
    yb = _dot(ob_ref[...], wob_ref[...])
    gl = gl_ref[...]
    mix = _sigmoid(gl[:, :d]) * ya + _sigmoid(gl[:, d:]) * yb
    out = _dot(mix.astype(BF16), wout_ref[...])
    xn = x_ref[...] + gate_ref[0] * out
    y = xn * lax.rsqrt(jnp.mean(xn * xn, axis=-1, keepdims=True) + EPS)
    o_ref[...] = y * fg_ref[...]


def _merge_out(oa, ob, pa, x2, gate, w_o_gdn, w_o_dsa, w_out, final_g, bsz, seq):
    d = D_MODEL
    tm = 512
    nt = seq // tm
    gl_off = (4 * GDN_W) // (2 * d)
    row = lambda b, i: (b * nt + i, 0)
    full = lambda b, i: (0, 0)
    return pl.pallas_call(
        _out_kernel,
        grid=(bsz, nt),
        in_specs=[pl.BlockSpec((tm, GDN_W), row),
                  pl.BlockSpec((tm, DSA_W), row),
                  pl.BlockSpec((tm, 2 * d), lambda b, i: (b * nt + i, gl_off)),
                  pl.BlockSpec((tm, d), row),
                  pl.BlockSpec((1, 1, d), lambda b, i: (b, 0, 0)),
                  pl.BlockSpec((GDN_W, d), full),
                  pl.BlockSpec((DSA_W, d), full),
                  pl.BlockSpec((d, d), full),
                  pl.BlockSpec((1, d), full)],
        out_specs=pl.BlockSpec((tm, d), row),
        out_shape=jax.ShapeDtypeStruct((bsz * seq, d), F32),
        compiler_params=pltpu.CompilerParams(vmem_limit_bytes=VMEM_LIMIT),
        name="merge_out",
    )(oa, ob, pa, x2, gate.reshape(bsz, 1, d), w_o_gdn, w_o_dsa, w_out,
      final_g.reshape(1, d))


def _split_w_in(w):
    o = 0
    names = ("qa", "ka", "va", "za", "aa", "ba", "qb", "ckv", "zb", "qi", "ki", "wi", "gl")
    sizes = (GDN_HEADS * GDN_DK, GDN_HEADS * GDN_DK, GDN_W, GDN_W, GDN_HEADS, GDN_HEADS,
             DSA_W, KV_RANK, DSA_W, IDX_HEADS * IDX_DIM, IDX_DIM, IDX_HEADS, 2 * D_MODEL)
    col = {}
    for nm, sz in zip(names, sizes):
        col[nm] = w[:, o:o + sz]
        o += sz
    k = w.shape[0]
    z = lambda n: jnp.zeros((k, n), w.dtype)
    wa = jnp.concatenate([col["qa"], col["ka"], col["va"], col["za"], col["gl"], col["zb"]], 1)
    n_misc = 2 * GDN_HEADS + IDX_HEADS
    wb = jnp.concatenate([col["ckv"], col["aa"], col["ba"], col["wi"], z(LANES - n_misc)], 1)
    pad = z(LANES - IDX_DIM)
    wc = jnp.concatenate([col["qb"], col["qi"], col["ki"], pad, pad, col["ki"]], 1)
    return wa.astype(BF16), wb.astype(BF16), wc.astype(BF16)


def kernel(x, c, norm_g, w_ada, b_ada, w_in, conv_w, a_log, dt_bias, gdn_norm_g, w_o_gdn,
           kv_norm_g, w_uk, w_uv, w_o_dsa, w_out, rel_bias, final_g):
    bsz, seq, d = x.shape
    depth = norm_g.shape[0]
    assert depth == 1, "the final RMSNorm is fused into the layer's merge kernel"
    x2 = x.reshape(bsz * seq, d)
    bias_tiles = _bias_tiles(rel_bias)
    for l in range(depth):
        mod = _adaln_mod(c, w_ada[l], b_ada[l])
        shift, scale, gate = mod[:, :d], mod[:, d:2 * d], mod[:, 2 * d:]
        h = _modulated_norm(x2, norm_g[l], scale, shift, seq)
        wa, wb, wc = _split_w_in(w_in[l])
        pa = _project(h, wa, F32, 512)
        pb = _project(h, wb, F32, wb.shape[1])
        pc = _project(h, wc, BF16, wc.shape[1] // 2)
        oa = _gdn(pa, pb, conv_w[l], a_log[l], dt_bias[l], gdn_norm_g[l], bsz, seq)
        wuk_t = jnp.transpose(w_uk[l], (1, 0, 2)).astype(BF16)
        wuv_t = jnp.transpose(w_uv[l], (1, 2, 0)).astype(BF16)
        ob = _dsa(pa, pb, pc, kv_norm_g[l], wuk_t, wuv_t, bias_tiles, bsz, seq)
        x2 = _merge_out(oa, ob, pa, x2, gate, w_o_gdn[l].astype(BF16),
                        w_o_dsa[l].astype(BF16), w_out[l].astype(BF16), final_g,
                        bsz, seq)
    return x2.reshape(bsz, seq, d)
```

```python
import math

import numpy as np
import jax
import jax.numpy as jnp
from jax import lax
from jax.experimental import pallas as pl
from jax.experimental.pallas import tpu as pltpu

D_MODEL = 1024
GDN_HEADS = 8
GDN_DK = 128
GDN_DV = 128
CONV_K = 4
DSA_HEADS = 8
DSA_DH = 128
KV_RANK = 256
IDX_HEADS = 8
IDX_DIM = 64
TOPK_MAX = 256
NUM_BUCKETS = 32
MAX_DISTANCE = 128
EPS = 1e-6

LANES = 128
SUBLANES = 8
PACKED_ROWS = 16
HALF16 = 1 << 15
GDN_GROUP = 8
DSA_SUPER = 4
GDN_W = GDN_HEADS * GDN_DV
DSA_W = DSA_HEADS * DSA_DH
LOG2E = math.log2(math.e)

F32 = jnp.float32
BF16 = jnp.bfloat16
INT_MIN = -2 ** 31
VMEM_LIMIT = 48 * 1024 * 1024

_NT = (((1,), (1,)), ((), ()))


def _dot(a, b):
    return jnp.dot(a, b, preferred_element_type=F32)


def _dot_nt(a, b):
    return lax.dot_general(a, b, _NT, preferred_element_type=F32)


def _split2(a):
    hi = a.astype(BF16)
    lo = (a - hi.astype(F32)).astype(BF16)
    return hi, lo


def _mm3(a, b):
    ah, al = _split2(a)
    bh, bl = _split2(b)
    n = b.shape[1]
    big = _dot(ah, jnp.concatenate([bh, bl], axis=1))
    return big[:, :n] + (big[:, n:] + _dot(al, bh))


def _sigmoid(x):
    return 0.5 * jnp.tanh(0.5 * x) + 0.5


def _silu(x):
    h = 0.5 * x
    return h * jnp.tanh(h) + h


def _softplus(x):
    return jnp.maximum(x, 0.0) + jnp.log1p(jnp.exp(-jnp.abs(x)))


def _fold_rows(x, op):
    parts = [x[r:r + SUBLANES, :] for r in range(0, x.shape[0], SUBLANES)]
    while len(parts) > 1:
        nxt = [op(parts[j], parts[j + 1]) for j in range(0, len(parts) - 1, 2)]
        if len(parts) % 2:
            nxt.append(parts[-1])
        parts = nxt
    return parts[0]


def _mod_kernel(c_ref, w_ref, b_ref, o_ref):
    a = _silu(c_ref[...])
    o_ref[...] = _mm3(a, w_ref[...]) + b_ref[...]


def _adaln_mod(c, w_ada, b_ada):
    bsz, d = c.shape
    n = w_ada.shape[1]
    tn = 512
    return pl.pallas_call(
        _mod_kernel,
        grid=(n // tn,),
        in_specs=[pl.BlockSpec((bsz, d), lambda j: (0, 0)),
                  pl.BlockSpec((d, tn), lambda j: (0, j)),
                  pl.BlockSpec((1, tn), lambda j: (0, j))],
        out_specs=pl.BlockSpec((bsz, tn), lambda j: (0, j)),
        out_shape=jax.ShapeDtypeStruct((bsz, n), F32),
        name="adaln_mod",
    )(c, w_ada, b_ada.reshape(1, n))


def _h_kernel(x_ref, g_ref, sc_ref, sh_ref, o_ref):
    x = x_ref[...]
    y = x * lax.rsqrt(jnp.mean(x * x, axis=-1, keepdims=True) + EPS)
    y = y * g_ref[...]
    o_ref[...] = (y * (1.0 + sc_ref[0]) + sh_ref[0]).astype(o_ref.dtype)


def _modulated_norm(x2, norm_g, scale, shift, seq):
    m, d = x2.shape
    bsz = m // seq
    tm = 512
    nt = seq // tm
    return pl.pallas_call(
        _h_kernel,
        grid=(bsz, nt),
        in_specs=[pl.BlockSpec((tm, d), lambda b, i: (b * nt + i, 0)),
                  pl.BlockSpec((1, d), lambda b, i: (0, 0)),
                  pl.BlockSpec((1, 1, d), lambda b, i: (b, 0, 0)),
                  pl.BlockSpec((1, 1, d), lambda b, i: (b, 0, 0))],
        out_specs=pl.BlockSpec((tm, d), lambda b, i: (b * nt + i, 0)),
        out_shape=jax.ShapeDtypeStruct((m, d), BF16),
        name="modulated_norm",
    )(x2, norm_g.reshape(1, d), scale.reshape(bsz, 1, d), shift.reshape(bsz, 1, d))


def _proj_kernel(h_ref, w_ref, o_ref):
    o_ref[...] = _dot(h_ref[...], w_ref[...]).astype(o_ref.dtype)


def _project(h, w, out_dtype, tn, tm=1024):
    m, k = h.shape
    n = w.shape[1]
    return pl.pallas_call(
        _proj_kernel,
        grid=(m // tm, n // tn),
        in_specs=[pl.BlockSpec((tm, k), lambda i, j: (i, 0)),
                  pl.BlockSpec((k, tn), lambda i, j: (0, j))],
        out_specs=pl.BlockSpec((tm, tn), lambda i, j: (i, j)),
        out_shape=jax.ShapeDtypeStruct((m, n), out_dtype),
        compiler_params=pltpu.CompilerParams(vmem_limit_bytes=VMEM_LIMIT),
        name="in_proj",
    )(h, w)


def _unit_lower_inverses(a_list):
    n = a_list[0].shape[0]
    ri = lax.broadcasted_iota(jnp.int32, (n, n), 0)
    ci = lax.broadcasted_iota(jnp.int32, (n, n), 1)
    x = ri ^ ci
    eye = (ri == ci).astype(F32)
    ts = [eye - jnp.where(x == 1, a, 0.0) for a in a_list]
    s = 2
    while s < n:
        level = (x >= s) & (x < 2 * s)
        tb = [t.astype(BF16) for t in ts]
        ps = [_dot(jnp.where(level, a, 0.0).astype(BF16), t) for a, t in zip(a_list, tb)]
        ts = [t - _dot(t16, p.astype(BF16)) for t, t16, p in zip(ts, tb, ps)]
        s *= 2
    return ts


def _refined_solves(t_list, a_list, rhs_list):
    tb = [t.astype(BF16) for t in t_list]
    x0 = [_dot(t, r.astype(BF16)) for t, r in zip(tb, rhs_list)]
    res = [r - (x + _mm3(a, x)) for r, x, a in zip(rhs_list, x0, a_list)]
    return [x + _dot(t, e.astype(BF16)) for x, t, e in zip(x0, tb, res)]


def _gdn_kernel(q_ref, k_ref, v_ref, z_ref, misc_ref, cwq_ref, cwk_ref, cwv_ref,
                alog_ref, dtb_ref, ng_ref, o_ref, qs, ks, vs, gs, bs,
                ous, cus, qws, nws, edl):
    hd = pl.program_id(1)
    seq = q_ref.shape[0]
    c = LANES
    row = lax.broadcasted_iota(jnp.int32, (seq, c), 0)
    lane = lax.broadcasted_iota(jnp.int32, (seq, c), 1)

    def conv_silu(u_ref, w_ref):
        u = u_ref[...]
        w = w_ref[...]

        def taps(x, causal_rows):
            y = x * w[CONV_K - 1:CONV_K, :]
            for k in range(1, CONV_K):
                sh = pltpu.roll(x, k, 0)
                if causal_rows is not None:
                    sh = jnp.where(causal_rows >= k, sh, 0.0)
                y = y + sh * w[CONV_K - 1 - k:CONV_K - k, :]
            return y

        head = taps(u[:SUBLANES, :], row[:SUBLANES, :])
        y = jnp.concatenate([head, taps(u, None)[SUBLANES:, :]], axis=0)
        return _silu(y)

    def l2n(x):
        return x * lax.rsqrt(jnp.sum(x * x, axis=-1, keepdims=True) + EPS)

    qs[...] = l2n(conv_silu(q_ref, cwq_ref)) * (GDN_DK ** -0.5)
    ks[...] = l2n(conv_silu(k_ref, cwk_ref))
    vs[...] = conv_silu(v_ref, cwv_ref)

    misc = misc_ref[...]
    aa = jnp.sum(jnp.where(lane == hd, misc, 0.0), axis=-1, keepdims=True)
    ba = jnp.sum(jnp.where(lane == hd + GDN_HEADS, misc, 0.0), axis=-1, keepdims=True)
    gs[...] = -jnp.exp(alog_ref[0]) * _softplus(aa + dtb_ref[0])
    bs[...] = jnp.broadcast_to(_sigmoid(ba), (seq, c))

    ri = lax.broadcasted_iota(jnp.int32, (c, c), 0)
    ci = lax.broadcasted_iota(jnp.int32, (c, c), 1)
    tri = (ri >= ci).astype(BF16)
    ng = ng_ref[...]

    def cumdecay(g):
        g1 = g.astype(BF16)
        r1 = g - g1.astype(F32)
        g2 = r1.astype(BF16)
        g3 = (r1 - g2.astype(F32)).astype(BF16)
        d = _dot(tri, jnp.concatenate([g1, g2, g3], axis=1))
        return d[:, :c] + (d[:, c:2 * c] + d[:, 2 * c:])

    def prepare_group(gi, carry):
        ns = [gi * GDN_GROUP + j for j in range(GDN_GROUP)]
        sls = [pl.ds(pl.multiple_of(n * c, c), c) for n in ns]
        dcols = [cumdecay(gs[sl, :]) for sl in sls]
        ks_ = [ks[sl, :] for sl in sls]
        kbs = [k * bs[sl, :] for k, sl in zip(ks_, sls)]
        kq = [_dot_nt(jnp.concatenate([kb, qs[sl, :]], axis=0).astype(BF16), k.astype(BF16))
              for kb, k, sl in zip(kbs, ks_, sls)]
        gammas = [jnp.exp(jnp.where(ri >= ci, d - d.T, -jnp.inf)) for d in dcols]
        a_list = [jnp.where(ri > ci, m[:c] * gm, 0.0) for m, gm in zip(kq, gammas)]
        attns = [(m[c:] * gm).astype(BF16) for m, gm in zip(kq, gammas)]
        ecols = [jnp.exp(d) for d in dcols]
        rhs = [jnp.concatenate([kb * e, vs[sl, :] * bs[sl, :]], axis=1)
               for sl, kb, e in zip(sls, kbs, ecols)]
        tinvs = _unit_lower_inverses(a_list)
        wu = [x.astype(BF16) for x in _refined_solves(tinvs, a_list, rhs)]
        kdts = [(k * jnp.exp(d[c - 1:c, :] - d)).T.astype(BF16) for k, d in zip(ks_, dcols)]
        aw = [_dot(at, x) for at, x in zip(attns, wu)]
        kw = [_dot(kd, x) for kd, x in zip(kdts, wu)]
        for n, sl, d, e, a_, k_ in zip(ns, sls, dcols, ecols, aw, kw):
            qws[sl, :] = (qs[sl, :] * e - a_[:, :c]).astype(BF16)
            ous[sl, :] = a_[:, c:]
            nws[sl, :] = k_[:, :c].astype(BF16)
            cus[sl, :] = k_[:, c:]
            edl[n] = jnp.broadcast_to(jnp.exp(d[c - 1:c, :]), (SUBLANES, c))
        return carry

    lax.fori_loop(0, seq // (c * GDN_GROUP), prepare_group, 0)

    def recur(n, state):
        sl = pl.ds(pl.multiple_of(n * c, c), c)
        sb = state.astype(BF16)
        ous[sl, :] = _dot(qws[sl, :], sb) + ous[sl, :]
        return state * edl[n][0:1, :] + (cus[sl, :] - _dot(nws[sl, :], sb))

    lax.fori_loop(0, seq // c, recur, jnp.zeros((GDN_DK, GDN_DV), F32))

    o = ous[...]
    on = o * lax.rsqrt(jnp.mean(o * o, axis=-1, keepdims=True) + EPS) * ng
    o_ref[...] = (on * _silu(z_ref[...])).astype(o_ref.dtype)


def _gdn(pa, pb, conv_w, a_log, dt_bias, gdn_norm_g, bsz, seq):
    nh = GDN_HEADS
    c = LANES
    alog_b = jnp.broadcast_to(a_log.reshape(nh, 1, 1), (nh, 1, c))
    dtb_b = jnp.broadcast_to(dt_bias.reshape(nh, 1, 1), (nh, 1, c))
    blk = lambda off: pl.BlockSpec((seq, c), lambda b, h: (b, off + h))
    cw = lambda off: pl.BlockSpec((CONV_K, c), lambda b, h: (0, off + h))
    hv = pl.BlockSpec((1, 1, c), lambda b, h: (h, 0, 0))
    return pl.pallas_call(
        _gdn_kernel,
        grid=(bsz, nh),
        in_specs=[blk(0), blk(nh), blk(2 * nh), blk(3 * nh),
                  pl.BlockSpec((seq, c), lambda b, h: (b, 2)),
                  cw(0), cw(nh), cw(2 * nh), hv, hv,
                  pl.BlockSpec((1, c), lambda b, h: (0, 0))],
        out_specs=pl.BlockSpec((seq, c), lambda b, h: (b, h)),
        out_shape=jax.ShapeDtypeStruct((bsz * seq, GDN_W), BF16),
        scratch_shapes=([pltpu.VMEM((seq, c), F32)] * 7
                        + [pltpu.VMEM((seq, c), BF16)] * 2
                        + [pltpu.VMEM((seq // c, SUBLANES, c), F32)]),
        compiler_params=pltpu.CompilerParams(vmem_limit_bytes=VMEM_LIMIT),
        name="gated_deltanet",
    )(pa, pa, pa, pa, pb, conv_w, conv_w, conv_w, alog_b, dtb_b,
      gdn_norm_g.reshape(1, c))


def _t5_bucket_table(n_tiles):
    s = np.arange(LANES)[:, None]
    q = np.arange(LANES)[None, :]
    out = []
    max_exact = NUM_BUCKETS // 2
    for d in range(n_tiles):
        n = np.maximum(d * LANES + q - s, 0)
        nf = np.maximum(n, 1).astype(np.float32)
        large = max_exact + (np.log(nf / max_exact) / math.log(MAX_DISTANCE / max_exact)
                             * (NUM_BUCKETS - max_exact)).astype(np.int32)
        large = np.minimum(large, NUM_BUCKETS - 1)
        out.append(np.where(n < max_exact, n, large))
    return np.stack(out).astype(np.int32)


_BIAS_TILES = 3
assert (_BIAS_TILES - 1) * LANES - (LANES - 1) >= MAX_DISTANCE


def _bias_kernel(bkt_ref, rel_ref, o_ref):
    hd = pl.program_id(0)
    for d in range(_BIAS_TILES):
        bkt = bkt_ref[d]
        acc = jnp.zeros((LANES, LANES), F32)
        for j in range(NUM_BUCKETS):
            acc = jnp.where(bkt == j, rel_ref[j, hd] * LOG2E, acc)
        o_ref[d, 0] = acc


def _bias_tiles(rel_bias):
    bkt = jnp.asarray(_t5_bucket_table(_BIAS_TILES))
    return pl.pallas_call(
        _bias_kernel,
        grid=(DSA_HEADS,),
        in_specs=[pl.BlockSpec((_BIAS_TILES, LANES, LANES), lambda h: (0, 0, 0)),
                  pl.BlockSpec(memory_space=pltpu.SMEM)],
        out_specs=pl.BlockSpec((_BIAS_TILES, 1, LANES, LANES), lambda h: (0, h, 0, 0)),
        out_shape=jax.ShapeDtypeStruct((_BIAS_TILES, DSA_HEADS, LANES, LANES), F32),
        name="t5_bias_tiles",
    )(bkt, rel_bias)


def _dsa_kernel(qb_ref, qi_ref, misc_ref, zb_ref, kil_ref, kih_ref, ckv_ref,
                kvg_ref, wuk_ref, wuvt_ref, bias_ref, o_ref,
                ckvn, ckvt, keys, khi, klo, madd, qlat, lg, pbuf, acc):
    i = pl.program_id(1)
    c = LANES
    sc_w = DSA_SUPER * c
    nsc = i // DSA_SUPER + 1
    nh = DSA_HEADS
    seq = ckv_ref.shape[0]

    @pl.when(i == 0)
    def _():
        x = ckv_ref[...]
        y = x * lax.rsqrt(jnp.mean(x * x, axis=-1, keepdims=True) + EPS) * kvg_ref[...]
        ckvn[...] = y.astype(BF16)
        for r in range(seq // c):
            ckvt[r // DSA_SUPER, :, (r % DSA_SUPER) * c:(r % DSA_SUPER + 1) * c] = (
                y[r * c:(r + 1) * c, :].T.astype(BF16))

    ri = lax.broadcasted_iota(jnp.int32, (c, c), 0)
    ci = lax.broadcasted_iota(jnp.int32, (c, c), 1)
    misc_t = misc_ref[...].T
    wscale = IDX_HEADS ** -0.5 * IDX_DIM ** -0.5
    wrow = [misc_t[2 * GDN_HEADS + h:2 * GDN_HEADS + h + 1, :] * wscale
            for h in range(IDX_HEADS)]
    npair = IDX_HEADS // 2
    qi4 = jnp.concatenate([qi_ref[:, p * c:(p + 1) * c] for p in range(npair)], axis=0)

    def score_super(s, carry):
        ccs = [s * DSA_SUPER + j for j in range(DSA_SUPER)]
        sls = [pl.ds(pl.multiple_of(cc * c, c), c) for cc in ccs]
        dots = [_dot_nt(jnp.concatenate([kil_ref[sl, :], kih_ref[sl, :]], axis=0), qi4)
                for sl in sls]
        for cc, d in zip(ccs, dots):
            acc_s = jnp.zeros((c, c), F32)
            for p in range(npair):
                acc_s = acc_s + wrow[2 * p] * jnp.maximum(d[:c, p * c:(p + 1) * c], 0.0)
                acc_s = acc_s + wrow[2 * p + 1] * jnp.maximum(d[c:, p * c:(p + 1) * c], 0.0)
            adm = (cc * c + ri) <= (i * c + ci)
            acc_s = jnp.where(adm, acc_s, -jnp.inf)
            bits = lax.bitcast_convert_type(acc_s, jnp.int32)
            key = jnp.where(bits < 0, bits ^ 0x7FFFFFFF, bits)
            keys[cc] = key
            khi[cc] = (key >> 16).astype(jnp.int16)
            klo[cc] = ((key & 0xFFFF) - HALF16).astype(jnp.int16)
        return carry

    lax.fori_loop(0, nsc, score_super, 0)

    one16 = jnp.ones((PACKED_ROWS, c), jnp.int16)
    zero16 = jnp.zeros((PACKED_ROWS, c), jnp.int16)

    def to16(v):
        return jnp.broadcast_to(v, (PACKED_ROWS, c)).astype(jnp.int16)

    def count16(plane, v, strict):
        v16 = to16(v)

        def body(s, cnt):
            parts = []
            for j in range(DSA_SUPER):
                for r in range(0, c, PACKED_ROWS):
                    x = plane[s * DSA_SUPER + j, r:r + PACKED_ROWS, :]
                    parts.append(jnp.where((x > v16) if strict else (x >= v16), one16, zero16))
            while len(parts) > 1:
                parts = [parts[k] + parts[k + 1] for k in range(0, len(parts), 2)]
            return cnt + parts[0]

        cnt = lax.fori_loop(0, nsc, body, zero16)
        return jnp.sum(cnt.astype(jnp.int32), axis=0, keepdims=True)

    def kth_largest16(plane, k):
        def bit_step(it, v):
            cand = v + jnp.left_shift(jnp.int32(1), 15 - it)
            return jnp.where(count16(plane, cand, False) >= k, cand, v)
        return lax.fori_loop(0, 16, bit_step, jnp.full((1, c), -HALF16, jnp.int32))

    k_all = jnp.full((1, c), TOPK_MAX, jnp.int32)
    thr_hi = kth_largest16(khi, k_all)
    k_lo = k_all - count16(khi, thr_hi, True)
    thr_hi_full = jnp.broadcast_to(thr_hi, (c, c)).astype(jnp.int16)

    def restrict_lo(cc, carry):
        klo[cc] = jnp.where(khi[cc] == thr_hi_full, klo[cc],
                            jnp.full((c, c), -HALF16, jnp.int16))
        return carry

    lax.fori_loop(0, nsc * DSA_SUPER, restrict_lo, 0)
    thr_lo = kth_largest16(klo, k_lo)
    thr = thr_hi * (2 * HALF16) + (thr_lo + HALF16)
    need = (k_lo - count16(klo, thr_lo, True)).astype(F32)
    lower = (ci < ri).astype(BF16)

    def mask_super(s, running):
        ccs = [s * DSA_SUPER + j for j in range(DSA_SUPER)]
        kcs = [keys[cc] for cc in ccs]
        eqs = [jnp.where(kc == thr, 1.0, 0.0) for kc in kcs]
        prefix = [_dot(lower, eq.astype(BF16)) for eq in eqs]
        totals = [jnp.sum(_fold_rows(eq, jnp.add), axis=0, keepdims=True) for eq in eqs]
        for cc, kc, eq, pre, tot in zip(ccs, kcs, eqs, prefix, totals):
            sel = jnp.where(kc > thr, 1.0, jnp.where(running + pre < need, eq, 0.0))
            adm = (cc * c + ri) <= (i * c + ci)
            madd[cc] = jnp.where(adm & (sel > 0.5), 0.0, -jnp.inf)
            running = running + tot
        return running

    lax.fori_loop(0, nsc, mask_super, jnp.zeros((1, c), F32))

    for h in range(nh):
        qh = qb_ref[:, h * DSA_DH:(h + 1) * DSA_DH]
        qlat[h * c:(h + 1) * c, :] = (
            _dot_nt(qh, wuk_ref[h]) * (DSA_DH ** -0.5 * LOG2E)).astype(BF16)

    acc[...] = jnp.zeros(acc.shape, F32)

    def logit_super(s, mx):
        kv = ckvn[pl.ds(pl.multiple_of(s * sc_w, sc_w), sc_w), :]
        l_all = _dot_nt(kv, qlat[...])
        new = []
        for h in range(nh):
            mh = mx[:, h * c:(h + 1) * c]
            for j in range(DSA_SUPER):
                cc = s * DSA_SUPER + j
                bidx = jnp.clip(i - cc, 0, _BIAS_TILES - 1)
                l = (l_all[j * c:(j + 1) * c, h * c:(h + 1) * c]
                     + bias_ref[bidx, h] + madd[cc])
                lg[s, j * c:(j + 1) * c, h * c:(h + 1) * c] = l
                mh = jnp.maximum(mh, _fold_rows(l, jnp.maximum))
            new.append(mh)
        return jnp.concatenate(new, axis=1)

    mx = lax.fori_loop(0, nsc, logit_super,
                       jnp.full((SUBLANES, nh * c), -jnp.inf, F32))
    m = jnp.max(mx, axis=0, keepdims=True)

    def pv_super(s, ssum):
        new = []
        for h in range(nh):
            sh = ssum[:, h * c:(h + 1) * c]
            mh = m[:, h * c:(h + 1) * c]
            for j in range(DSA_SUPER):
                p = jnp.exp2(lg[s, j * c:(j + 1) * c, h * c:(h + 1) * c] - mh)
                sh = sh + _fold_rows(p, jnp.add)
                pbuf[j * c:(j + 1) * c, h * c:(h + 1) * c] = p.astype(BF16)
            new.append(sh)
        acc[...] = acc[...] + _dot(ckvt[s], pbuf[...])
        return jnp.concatenate(new, axis=1)

    ssum = lax.fori_loop(0, nsc, pv_super, jnp.zeros((SUBLANES, nh * c), F32))
    denom = jnp.sum(ssum, axis=0, keepdims=True)

    for h in range(nh):
        cols = slice(h * c, (h + 1) * c)
        olat_t = (acc[:, cols] / denom[:, cols]).astype(BF16)
        ob = _dot(wuvt_ref[h], olat_t).T
        zb = zb_ref[:, h * DSA_DH:(h + 1) * DSA_DH]
        o_ref[:, h * DSA_DH:(h + 1) * DSA_DH] = (ob * _silu(zb)).astype(o_ref.dtype)


def _dsa(pa, pb, pc, kv_norm_g, wuk_t, wuv_t, bias_tiles, bsz, seq):
    c = LANES
    nb = seq // c
    nsc = nb // DSA_SUPER
    rows_all = DSA_HEADS * c
    za_off = (4 * GDN_W + 2 * D_MODEL) // DSA_W
    qi_off = DSA_W // (IDX_HEADS * IDX_DIM)
    kil_off = (DSA_W + IDX_HEADS * IDX_DIM) // c
    row = lambda b, i: b * nb + i
    return pl.pallas_call(
        _dsa_kernel,
        grid=(bsz, nb),
        in_specs=[
            pl.BlockSpec((c, DSA_W), lambda b, i: (row(b, i), 0)),
            pl.BlockSpec((c, IDX_HEADS * IDX_DIM), lambda b, i: (row(b, i), qi_off)),
            pl.BlockSpec((c, c), lambda b, i: (row(b, i), 2)),
            pl.BlockSpec((c, DSA_W), lambda b, i: (row(b, i), za_off)),
            pl.BlockSpec((seq, c), lambda b, i: (b, kil_off)),
            pl.BlockSpec((seq, c), lambda b, i: (b, kil_off + 1)),
            pl.BlockSpec((seq, KV_RANK), lambda b, i: (b, 0)),
            pl.BlockSpec((1, KV_RANK), lambda b, i: (0, 0)),
            pl.BlockSpec((DSA_HEADS, KV_RANK, DSA_DH), lambda b, i: (0, 0, 0)),
            pl.BlockSpec((DSA_HEADS, DSA_DH, KV_RANK), lambda b, i: (0, 0, 0)),
            pl.BlockSpec((_BIAS_TILES, DSA_HEADS, c, c), lambda b, i: (0, 0, 0, 0)),
        ],
        out_specs=pl.BlockSpec((c, DSA_W), lambda b, i: (row(b, i), 0)),
        out_shape=jax.ShapeDtypeStruct((bsz * seq, DSA_W), BF16),
        scratch_shapes=[pltpu.VMEM((seq, KV_RANK), BF16),
                        pltpu.VMEM((nsc, KV_RANK, DSA_SUPER * c), BF16),
                        pltpu.VMEM((nb, c, c), jnp.int32),
                        pltpu.VMEM((nb, c, c), jnp.int16),
                        pltpu.VMEM((nb, c, c), jnp.int16),
                        pltpu.VMEM((nb, c, c), F32),
                        pltpu.VMEM((rows_all, KV_RANK), BF16),
                        pltpu.VMEM((nsc, DSA_SUPER * c, rows_all), F32),
                        pltpu.VMEM((DSA_SUPER * c, rows_all), BF16),
                        pltpu.VMEM((KV_RANK, rows_all), F32)],
        compiler_params=pltpu.CompilerParams(vmem_limit_bytes=VMEM_LIMIT),
        name="dsa_attention",
    )(pc, pc, pb, pa, pc, pc, pb, kv_norm_g.reshape(1, KV_RANK), wuk_t, wuv_t,
      bias_tiles)


def _out_kernel(oa_ref, ob_ref, gl_ref, x_ref, gate_ref, woa_ref, wob_ref, wout_ref,
                fg_ref, o_ref):
    d = x_ref.shape[1]
    ya = _dot(oa_ref[...], woa_ref[...])
    yb = _dot(ob_ref[...], wob_ref[...])
    gl = gl_ref[...]
    mix = _sigmoid(gl[:, :d]) * ya + _sigmoid(gl[:, d:]) * yb
    out = _dot(mix.astype(BF16), wout_ref[...])
    xn = x_ref[...] + gate_ref[0] * out
    y = xn * lax.rsqrt(jnp.mean(xn * xn, axis=-1, keepdims=True) + EPS)
    o_ref[...] = y * fg_ref[...]


def _merge_out(oa, ob, pa, x2, gate, w_o_gdn, w_o_dsa, w_out, final_g, bsz, seq):
    d = D_MODEL
    tm = 512
    nt = seq // tm
    gl_off = (4 * GDN_W) // (2 * d)
    row = lambda b, i: (b * nt + i, 0)
    full = lambda b, i: (0, 0)
    return pl.pallas_call(
        _out_kernel,
        grid=(bsz, nt),
        in_specs=[pl.BlockSpec((tm, GDN_W), row),
                  pl.BlockSpec((tm, DSA_W), row),
                  pl.BlockSpec((tm, 2 * d), lambda b, i: (b * nt + i, gl_off)),
                  pl.BlockSpec((tm, d), row),
                  pl.BlockSpec((1, 1, d), lambda b, i: (b, 0, 0)),
                  pl.BlockSpec((GDN_W, d), full),
                  pl.BlockSpec((DSA_W, d), full),
                  pl.BlockSpec((d, d), full),
                  pl.BlockSpec((1, d), full)],
        out_specs=pl.BlockSpec((tm, d), row),
        out_shape=jax.ShapeDtypeStruct((bsz * seq, d), F32),
        compiler_params=pltpu.CompilerParams(vmem_limit_bytes=VMEM_LIMIT),
        name="merge_out",
    )(oa, ob, pa, x2, gate.reshape(bsz, 1, d), w_o_gdn, w_o_dsa, w_out,
      final_g.reshape(1, d))


def _split_w_in(w):
    o = 0
    names = ("qa", "ka", "va", "za", "aa", "ba", "qb", "ckv", "zb", "qi", "ki", "wi", "gl")
    sizes = (GDN_HEADS * GDN_DK, GDN_HEADS * GDN_DK, GDN_W, GDN_W, GDN_HEADS, GDN_HEADS,
             DSA_W, KV_RANK, DSA_W, IDX_HEADS * IDX_DIM, IDX_DIM, IDX_HEADS, 2 * D_MODEL)
    col = {}
    for nm, sz in zip(names, sizes):
        col[nm] = w[:, o:o + sz]
        o += sz
    k = w.shape[0]
    z = lambda n: jnp.zeros((k, n), w.dtype)
    wa = jnp.concatenate([col["qa"], col["ka"], col["va"], col["za"], col["gl"], col["zb"]], 1)
    n_misc = 2 * GDN_HEADS + IDX_HEADS
    wb = jnp.concatenate([col["ckv"], col["aa"], col["ba"], col["wi"], z(LANES - n_misc)], 1)
    pad = z(LANES - IDX_DIM)
    wc = jnp.concatenate([col["qb"], col["qi"], col["ki"], pad, pad, col["ki"]], 1)
    return wa.astype(BF16), wb.astype(BF16), wc.astype(BF16)


def kernel(x, c, norm_g, w_ada, b_ada, w_in, conv_w, a_log, dt_bias, gdn_norm_g, w_o_gdn,
           kv_norm_g, w_uk, w_uv, w_o_dsa, w_out, rel_bias, final_g):
    bsz, seq, d = x.shape
    depth = norm_g.shape[0]
    assert depth == 1, "the final RMSNorm is fused into the layer's merge kernel"
    x2 = x.reshape(bsz * seq, d)
    bias_tiles = _bias_tiles(rel_bias)
    for l in range(depth):
        mod = _adaln_mod(c, w_ada[l], b_ada[l])
        shift, scale, gate = mod[:, :d], mod[:, d:2 * d], mod[:, 2 * d:]
        h = _modulated_norm(x2, norm_g[l], scale, shift, seq)
        wa, wb, wc = _split_w_in(w_in[l])
        pa = _project(h, wa, F32, 512, tm=2048)
        pb = _project(h, wb, F32, wb.shape[1])
        pc = _project(h, wc, BF16, wc.shape[1] // 2)
        oa = _gdn(pa, pb, conv_w[l], a_log[l], dt_bias[l], gdn_norm_g[l], bsz, seq)
        wuk_t = jnp.transpose(w_uk[l], (1, 0, 2)).astype(BF16)
        wuv_t = jnp.transpose(w_uv[l], (1, 2, 0)).astype(BF16)
        ob = _dsa(pa, pb, pc, kv_norm_g[l], wuk_t, wuv_t, bias_tiles, bsz, seq)
        x2 = _merge_out(oa, ob, pa, x2, gate, w_o_gdn[l].astype(BF16),
                        w_o_dsa[l].astype(BF16), w_out[l].astype(BF16), final_g,
                        bsz, seq)
    return x2.reshape(bsz, seq, d)
```

```python
import math

import numpy as np
import jax
import jax.numpy as jnp
from jax import lax
from jax.experimental import pallas as pl
from jax.experimental.pallas import tpu as pltpu

D_MODEL = 1024
GDN_HEADS = 8
GDN_DK = 128
GDN_DV = 128
CONV_K = 4
DSA_HEADS = 8
DSA_DH = 128
KV_RANK = 256
IDX_HEADS = 8
IDX_DIM = 64
TOPK_MAX = 256
NUM_BUCKETS = 32
MAX_DISTANCE = 128
EPS = 1e-6

LANES = 128
SUBLANES = 8
GDN_GROUP = 8
DSA_SUPER = 4
GDN_W = GDN_HEADS * GDN_DV
DSA_W = DSA_HEADS * DSA_DH
LOG2E = math.log2(math.e)

F32 = jnp.float32
BF16 = jnp.bfloat16
INT_MIN = -2 ** 31
VMEM_LIMIT = 48 * 1024 * 1024

_NT = (((1,), (1,)), ((), ()))


def _dot(a, b):
    return jnp.dot(a, b, preferred_element_type=F32)


def _dot_nt(a, b):
    return lax.dot_general(a, b, _NT, preferred_element_type=F32)


def _split2(a):
    hi = a.astype(BF16)
    lo = (a - hi.astype(F32)).astype(BF16)
    return hi, lo


def _mm3(a, b):
    ah, al = _split2(a)
    bh, bl = _split2(b)
    n = b.shape[1]
    big = _dot(ah, jnp.concatenate([bh, bl], axis=1))
    return big[:, :n] + (big[:, n:] + _dot(al, bh))


def _sigmoid(x):
    return 0.5 * jnp.tanh(0.5 * x) + 0.5


def _silu(x):
    h = 0.5 * x
    return h * jnp.tanh(h) + h


def _softplus(x):
    return jnp.maximum(x, 0.0) + jnp.log1p(jnp.exp(-jnp.abs(x)))


def _fold_rows(x, op):
    parts = [x[r:r + SUBLANES, :] for r in range(0, x.shape[0], SUBLANES)]
    while len(parts) > 1:
        nxt = [op(parts[j], parts[j + 1]) for j in range(0, len(parts) - 1, 2)]
        if len(parts) % 2:
            nxt.append(parts[-1])
        parts = nxt
    return parts[0]


def _mod_kernel(c_ref, w_ref, b_ref, o_ref):
    a = _silu(c_ref[...])
    o_ref[...] = _mm3(a, w_ref[...]) + b_ref[...]


def _adaln_mod(c, w_ada, b_ada):
    bsz, d = c.shape
    n = w_ada.shape[1]
    tn = 512
    return pl.pallas_call(
        _mod_kernel,
        grid=(n // tn,),
        in_specs=[pl.BlockSpec((bsz, d), lambda j: (0, 0)),
                  pl.BlockSpec((d, tn), lambda j: (0, j)),
                  pl.BlockSpec((1, tn), lambda j: (0, j))],
        out_specs=pl.BlockSpec((bsz, tn), lambda j: (0, j)),
        out_shape=jax.ShapeDtypeStruct((bsz, n), F32),
        name="adaln_mod",
    )(c, w_ada, b_ada.reshape(1, n))


def _h_kernel(x_ref, g_ref, sc_ref, sh_ref, o_ref):
    x = x_ref[...]
    y = x * lax.rsqrt(jnp.mean(x * x, axis=-1, keepdims=True) + EPS)
    y = y * g_ref[...]
    o_ref[...] = (y * (1.0 + sc_ref[0]) + sh_ref[0]).astype(o_ref.dtype)


def _modulated_norm(x2, norm_g, scale, shift, seq):
    m, d = x2.shape
    bsz = m // seq
    tm = 512
    nt = seq // tm
    return pl.pallas_call(
        _h_kernel,
        grid=(bsz, nt),
        in_specs=[pl.BlockSpec((tm, d), lambda b, i: (b * nt + i, 0)),
                  pl.BlockSpec((1, d), lambda b, i: (0, 0)),
                  pl.BlockSpec((1, 1, d), lambda b, i: (b, 0, 0)),
                  pl.BlockSpec((1, 1, d), lambda b, i: (b, 0, 0))],
        out_specs=pl.BlockSpec((tm, d), lambda b, i: (b * nt + i, 0)),
        out_shape=jax.ShapeDtypeStruct((m, d), BF16),
        name="modulated_norm",
    )(x2, norm_g.reshape(1, d), scale.reshape(bsz, 1, d), shift.reshape(bsz, 1, d))


def _proj_kernel(h_ref, w_ref, o_ref):
    o_ref[...] = _dot(h_ref[...], w_ref[...]).astype(o_ref.dtype)


def _project(h, w, out_dtype, tn, tm=1024):
    m, k = h.shape
    n = w.shape[1]
    return pl.pallas_call(
        _proj_kernel,
        grid=(m // tm, n // tn),
        in_specs=[pl.BlockSpec((tm, k), lambda i, j: (i, 0)),
                  pl.BlockSpec((k, tn), lambda i, j: (0, j))],
        out_specs=pl.BlockSpec((tm, tn), lambda i, j: (i, j)),
        out_shape=jax.ShapeDtypeStruct((m, n), out_dtype),
        compiler_params=pltpu.CompilerParams(vmem_limit_bytes=VMEM_LIMIT),
        name="in_proj",
    )(h, w)


def _unit_lower_inverses(a_list):
    n = a_list[0].shape[0]
    ri = lax.broadcasted_iota(jnp.int32, (n, n), 0)
    ci = lax.broadcasted_iota(jnp.int32, (n, n), 1)
    x = ri ^ ci
    eye = (ri == ci).astype(F32)
    ts = [eye - jnp.where(x == 1, a, 0.0) for a in a_list]
    s = 2
    while s < n:
        level = (x >= s) & (x < 2 * s)
        tb = [t.astype(BF16) for t in ts]
        ps = [_dot(jnp.where(level, a, 0.0).astype(BF16), t) for a, t in zip(a_list, tb)]
        ts = [t - _dot(t16, p.astype(BF16)) for t, t16, p in zip(ts, tb, ps)]
        s *= 2
    return ts


def _refined_solves(t_list, a_list, rhs_list):
    tb = [t.astype(BF16) for t in t_list]
    x0 = [_dot(t, r.astype(BF16)) for t, r in zip(tb, rhs_list)]
    res = [r - (x + _mm3(a, x)) for r, x, a in zip(rhs_list, x0, a_list)]
    return [x + _dot(t, e.astype(BF16)) for x, t, e in zip(x0, tb, res)]


def _gdn_kernel(q_ref, k_ref, v_ref, misc_ref, cwq_ref, cwk_ref, cwv_ref,
                alog_ref, dtb_ref, z_ref, ng_ref, o_ref, qs, ks, vs, gs, bs,
                ous, cus, qws, nws, edl):
    s = pl.program_id(0)
    n_items = pl.num_programs(0) - 1
    cur = (s + 1) % 2
    nxt = s % 2
    hd = jnp.minimum(s, n_items - 1) % GDN_HEADS
    seq = q_ref.shape[0]
    c = LANES
    nchunk = seq // c
    row8 = lax.broadcasted_iota(jnp.int32, (SUBLANES, c), 0)
    lane = lax.broadcasted_iota(jnp.int32, (c, c), 1)
    ri = lax.broadcasted_iota(jnp.int32, (c, c), 0)
    ci = lane
    tri = (ri >= ci).astype(BF16)

    @pl.when(s == 0)
    def _():
        for ref in (qs, ks, vs, gs, bs):
            ref[cur] = jnp.zeros((seq, c), F32)

    def prologue(n):
        sl = pl.ds(pl.multiple_of(n * c, c), c)
        prev = pl.ds(pl.multiple_of(jnp.maximum(n * c - SUBLANES, 0), SUBLANES), SUBLANES)

        def conv_silu(u_ref, w_ref):
            u = u_ref[sl, :]
            w = w_ref[...]
            tail = jnp.where(n > 0, u_ref[prev, :], 0.0)
            y = u * w[CONV_K - 1:CONV_K, :]
            for k in range(1, CONV_K):
                sh = pltpu.roll(u, k, 0)
                head = jnp.where(row8 < k, pltpu.roll(tail, k, 0), sh[:SUBLANES, :])
                sh = jnp.concatenate([head, sh[SUBLANES:, :]], axis=0)
                y = y + sh * w[CONV_K - 1 - k:CONV_K - k, :]
            return _silu(y)

        def l2n(x):
            return x * lax.rsqrt(jnp.sum(x * x, axis=-1, keepdims=True) + EPS)

        qs[nxt, sl, :] = l2n(conv_silu(q_ref, cwq_ref)) * (GDN_DK ** -0.5)
        ks[nxt, sl, :] = l2n(conv_silu(k_ref, cwk_ref))
        vs[nxt, sl, :] = conv_silu(v_ref, cwv_ref)
        misc = misc_ref[sl, :]
        aa = jnp.sum(jnp.where(lane == hd, misc, 0.0), axis=-1, keepdims=True)
        ba = jnp.sum(jnp.where(lane == hd + GDN_HEADS, misc, 0.0), axis=-1, keepdims=True)
        gs[nxt, sl, :] = -jnp.exp(alog_ref[0]) * _softplus(aa + dtb_ref[0])
        bs[nxt, sl, :] = jnp.broadcast_to(_sigmoid(ba), (c, c))

    def cumdecay(g):
        g1 = g.astype(BF16)
        r1 = g - g1.astype(F32)
        g2 = r1.astype(BF16)
        g3 = (r1 - g2.astype(F32)).astype(BF16)
        d = _dot(tri, jnp.concatenate([g1, g2, g3], axis=1))
        return d[:, :c] + (d[:, c:2 * c] + d[:, 2 * c:])

    n_groups = nchunk // GDN_GROUP
    pro_in_prepare = nchunk // (2 * n_groups)

    def prepare_group(gi, carry):
        ns = [gi * GDN_GROUP + j for j in range(GDN_GROUP)]
        sls = [pl.ds(pl.multiple_of(n * c, c), c) for n in ns]
        dcols = [cumdecay(gs[cur, sl, :]) for sl in sls]
        ks_ = [ks[cur, sl, :] for sl in sls]
        kbs = [k * bs[cur, sl, :] for k, sl in zip(ks_, sls)]
        kq = [_dot_nt(jnp.concatenate([kb, qs[cur, sl, :]], axis=0).astype(BF16),
                      k.astype(BF16))
              for kb, k, sl in zip(kbs, ks_, sls)]
        gammas = [jnp.exp(jnp.where(ri >= ci, d - d.T, -jnp.inf)) for d in dcols]
        a_list = [jnp.where(ri > ci, m[:c] * gm, 0.0) for m, gm in zip(kq, gammas)]
        attns = [(m[c:] * gm).astype(BF16) for m, gm in zip(kq, gammas)]
        ecols = [jnp.exp(d) for d in dcols]
        rhs = [jnp.concatenate([kb * e, vs[cur, sl, :] * bs[cur, sl, :]], axis=1)
               for sl, kb, e in zip(sls, kbs, ecols)]
        tinvs = _unit_lower_inverses(a_list)
        wu = [x.astype(BF16) for x in _refined_solves(tinvs, a_list, rhs)]
        kdts = [(k * jnp.exp(d[c - 1:c, :] - d)).T.astype(BF16) for k, d in zip(ks_, dcols)]
        aw = [_dot(at, x) for at, x in zip(attns, wu)]
        kw = [_dot(kd, x) for kd, x in zip(kdts, wu)]
        for n, sl, d, e, a_, k_ in zip(ns, sls, dcols, ecols, aw, kw):
            qws[sl, :] = (qs[cur, sl, :] * e - a_[:, :c]).astype(BF16)
            ous[sl, :] = a_[:, c:]
            nws[sl, :] = k_[:, :c].astype(BF16)
            cus[sl, :] = k_[:, c:]
            edl[n] = jnp.broadcast_to(jnp.exp(d[c - 1:c, :]), (SUBLANES, c))
        for j in range(pro_in_prepare):
            prologue(gi * pro_in_prepare + j)
        return carry

    lax.fori_loop(0, n_groups, prepare_group, 0)

    def recur_pair(it, state):
        for j in range(2):
            n = 2 * it + j
            sl = pl.ds(pl.multiple_of(n * c, c), c)
            sb = state.astype(BF16)
            ous[sl, :] = _dot(qws[sl, :], sb) + ous[sl, :]
            state = state * edl[n][0:1, :] + (cus[sl, :] - _dot(nws[sl, :], sb))
        prologue(nchunk // 2 + it)
        return state

    lax.fori_loop(0, nchunk // 2, recur_pair, jnp.zeros((GDN_DK, GDN_DV), F32))

    o = ous[...]
    on = o * lax.rsqrt(jnp.mean(o * o, axis=-1, keepdims=True) + EPS) * ng_ref[...]
    o_ref[...] = (on * _silu(z_ref[...])).astype(o_ref.dtype)


def _gdn(pa, pb, conv_w, a_log, dt_bias, gdn_norm_g, bsz, seq):
    nh = GDN_HEADS
    c = LANES
    n_items = bsz * nh
    assert (seq // c) % (2 * GDN_GROUP) == 0
    alog_b = jnp.broadcast_to(a_log.reshape(nh, 1, 1), (nh, 1, c))
    dtb_b = jnp.broadcast_to(dt_bias.reshape(nh, 1, 1), (nh, 1, c))
    nxt = lambda s: jnp.minimum(s, n_items - 1)
    cur = lambda s: jnp.maximum(s - 1, 0)
    blk = lambda off: pl.BlockSpec((seq, c), lambda s: (nxt(s) // nh, off + nxt(s) % nh))
    cw = lambda off: pl.BlockSpec((CONV_K, c), lambda s: (0, off + nxt(s) % nh))
    hv = pl.BlockSpec((1, 1, c), lambda s: (nxt(s) % nh, 0, 0))
    return pl.pallas_call(
        _gdn_kernel,
        grid=(n_items + 1,),
        in_specs=[blk(0), blk(nh), blk(2 * nh),
                  pl.BlockSpec((seq, c), lambda s: (nxt(s) // nh, 2)),
                  cw(0), cw(nh), cw(2 * nh), hv, hv,
                  pl.BlockSpec((seq, c), lambda s: (cur(s) // nh, 3 * nh + cur(s) % nh)),
                  pl.BlockSpec((1, c), lambda s: (0, 0))],
        out_specs=pl.BlockSpec((seq, c), lambda s: (cur(s) // nh, cur(s) % nh)),
        out_shape=jax.ShapeDtypeStruct((bsz * seq, GDN_W), BF16),
        scratch_shapes=([pltpu.VMEM((2, seq, c), F32)] * 5
                        + [pltpu.VMEM((seq, c), F32)] * 2
                        + [pltpu.VMEM((seq, c), BF16)] * 2
                        + [pltpu.VMEM((seq // c, SUBLANES, c), F32)]),
        compiler_params=pltpu.CompilerParams(vmem_limit_bytes=VMEM_LIMIT),
        name="gated_deltanet",
    )(pa, pa, pa, pb, conv_w, conv_w, conv_w, alog_b, dtb_b, pa,
      gdn_norm_g.reshape(1, c))


def _t5_bucket_table(n_tiles):
    s = np.arange(LANES)[:, None]
    q = np.arange(LANES)[None, :]
    out = []
    max_exact = NUM_BUCKETS // 2
    for d in range(n_tiles):
        n = np.maximum(d * LANES + q - s, 0)
        nf = np.maximum(n, 1).astype(np.float32)
        large = max_exact + (np.log(nf / max_exact) / math.log(MAX_DISTANCE / max_exact)
                             * (NUM_BUCKETS - max_exact)).astype(np.int32)
        large = np.minimum(large, NUM_BUCKETS - 1)
        out.append(np.where(n < max_exact, n, large))
    return np.stack(out).astype(np.int32)


_BIAS_TILES = 3
assert (_BIAS_TILES - 1) * LANES - (LANES - 1) >= MAX_DISTANCE


def _bias_kernel(bkt_ref, rel_ref, o_ref):
    hd = pl.program_id(0)
    for d in range(_BIAS_TILES):
        bkt = bkt_ref[d]
        acc = jnp.zeros((LANES, LANES), F32)
        for j in range(NUM_BUCKETS):
            acc = jnp.where(bkt == j, rel_ref[j, hd] * LOG2E, acc)
        o_ref[d, 0] = acc


def _bias_tiles(rel_bias):
    bkt = jnp.asarray(_t5_bucket_table(_BIAS_TILES))
    return pl.pallas_call(
        _bias_kernel,
        grid=(DSA_HEADS,),
        in_specs=[pl.BlockSpec((_BIAS_TILES, LANES, LANES), lambda h: (0, 0, 0)),
                  pl.BlockSpec(memory_space=pltpu.SMEM)],
        out_specs=pl.BlockSpec((_BIAS_TILES, 1, LANES, LANES), lambda h: (0, h, 0, 0)),
        out_shape=jax.ShapeDtypeStruct((_BIAS_TILES, DSA_HEADS, LANES, LANES), F32),
        name="t5_bias_tiles",
    )(bkt, rel_bias)


def _dsa_kernel(qb_ref, qi_ref, misc_ref, zb_ref, kil_ref, kih_ref, ckv_ref,
                kvg_ref, wuk_ref, wuvt_ref, bias_ref, o_ref,
                ckvn, ckvt, keys, madd, qlat, lg, pbuf, acc):
    i = pl.program_id(1)
    c = LANES
    sc_w = DSA_SUPER * c
    nsc = i // DSA_SUPER + 1
    nh = DSA_HEADS
    seq = ckv_ref.shape[0]

    @pl.when(i == 0)
    def _():
        x = ckv_ref[...]
        y = x * lax.rsqrt(jnp.mean(x * x, axis=-1, keepdims=True) + EPS) * kvg_ref[...]
        ckvn[...] = y.astype(BF16)
        for r in range(seq // c):
            ckvt[r // DSA_SUPER, :, (r % DSA_SUPER) * c:(r % DSA_SUPER + 1) * c] = (
                y[r * c:(r + 1) * c, :].T.astype(BF16))

    ri = lax.broadcasted_iota(jnp.int32, (c, c), 0)
    ci = lax.broadcasted_iota(jnp.int32, (c, c), 1)
    misc_t = misc_ref[...].T
    wscale = IDX_HEADS ** -0.5 * IDX_DIM ** -0.5
    wrow = [misc_t[2 * GDN_HEADS + h:2 * GDN_HEADS + h + 1, :] * wscale
            for h in range(IDX_HEADS)]
    npair = IDX_HEADS // 2
    qi4 = jnp.concatenate([qi_ref[:, p * c:(p + 1) * c] for p in range(npair)], axis=0)

    def score_super(s, carry):
        ccs = [s * DSA_SUPER + j for j in range(DSA_SUPER)]
        sls = [pl.ds(pl.multiple_of(cc * c, c), c) for cc in ccs]
        dots = [_dot_nt(jnp.concatenate([kil_ref[sl, :], kih_ref[sl, :]], axis=0), qi4)
                for sl in sls]
        for cc, d in zip(ccs, dots):
            acc_s = jnp.zeros((c, c), F32)
            for p in range(npair):
                acc_s = acc_s + wrow[2 * p] * jnp.maximum(d[:c, p * c:(p + 1) * c], 0.0)
                acc_s = acc_s + wrow[2 * p + 1] * jnp.maximum(d[c:, p * c:(p + 1) * c], 0.0)
            adm = (cc * c + ri) <= (i * c + ci)
            acc_s = jnp.where(adm, acc_s, -jnp.inf)
            bits = lax.bitcast_convert_type(acc_s, jnp.int32)
            keys[cc] = jnp.where(bits < 0, bits ^ 0x7FFFFFFF, bits)
        return carry

    lax.fori_loop(0, nsc, score_super, 0)

    kf = float(TOPK_MAX)

    def count(pred):
        def body(s, cnt):
            parts = [_fold_rows(jnp.where(pred(keys[s * DSA_SUPER + j]), 1.0, 0.0), jnp.add)
                     for j in range(DSA_SUPER)]
            return cnt + ((parts[0] + parts[1]) + (parts[2] + parts[3]))
        cnt = lax.fori_loop(0, nsc, body, jnp.zeros((SUBLANES, c), F32))
        return jnp.sum(cnt, axis=0, keepdims=True)

    def bit_step(it, v):
        cand = v + jnp.left_shift(jnp.int32(1), 31 - it)
        tot = count(lambda kc: kc >= cand)
        return jnp.where(tot >= kf, cand, v)

    thr = lax.fori_loop(0, 32, bit_step, jnp.full((1, c), INT_MIN, jnp.int32))
    need = kf - count(lambda kc: kc > thr)
    lower = (ci < ri).astype(BF16)

    def mask_super(s, running):
        ccs = [s * DSA_SUPER + j for j in range(DSA_SUPER)]
        kcs = [keys[cc] for cc in ccs]
        eqs = [jnp.where(kc == thr, 1.0, 0.0) for kc in kcs]
        prefix = [_dot(lower, eq.astype(BF16)) for eq in eqs]
        totals = [jnp.sum(_fold_rows(eq, jnp.add), axis=0, keepdims=True) for eq in eqs]
        for cc, kc, eq, pre, tot in zip(ccs, kcs, eqs, prefix, totals):
            sel = jnp.where(kc > thr, 1.0, jnp.where(running + pre < need, eq, 0.0))
            adm = (cc * c + ri) <= (i * c + ci)
            madd[cc] = jnp.where(adm & (sel > 0.5), 0.0, -jnp.inf)
            running = running + tot
        return running

    lax.fori_loop(0, nsc, mask_super, jnp.zeros((1, c), F32))

    for h in range(nh):
        qh = qb_ref[:, h * DSA_DH:(h + 1) * DSA_DH]
        qlat[h * c:(h + 1) * c, :] = (
            _dot_nt(qh, wuk_ref[h]) * (DSA_DH ** -0.5 * LOG2E)).astype(BF16)

    acc[...] = jnp.zeros(acc.shape, F32)

    def logit_super(s, mx):
        kv = ckvn[pl.ds(pl.multiple_of(s * sc_w, sc_w), sc_w), :]
        l_all = _dot_nt(kv, qlat[...])
        new = []
        for h in range(nh):
            mh = mx[:, h * c:(h + 1) * c]
            for j in range(DSA_SUPER):
                cc = s * DSA_SUPER + j
                bidx = jnp.clip(i - cc, 0, _BIAS_TILES - 1)
                l = (l_all[j * c:(j + 1) * c, h * c:(h + 1) * c]
                     + bias_ref[bidx, h] + madd[cc])
                lg[s, j * c:(j + 1) * c, h * c:(h + 1) * c] = l
                mh = jnp.maximum(mh, _fold_rows(l, jnp.maximum))
            new.append(mh)
        return jnp.concatenate(new, axis=1)

    mx = lax.fori_loop(0, nsc, logit_super,
                       jnp.full((SUBLANES, nh * c), -jnp.inf, F32))
    m = jnp.max(mx, axis=0, keepdims=True)

    def pv_super(s, ssum):
        new = []
        for h in range(nh):
            sh = ssum[:, h * c:(h + 1) * c]
            mh = m[:, h * c:(h + 1) * c]
            for j in range(DSA_SUPER):
                p = jnp.exp2(lg[s, j * c:(j + 1) * c, h * c:(h + 1) * c] - mh)
                sh = sh + _fold_rows(p, jnp.add)
                pbuf[j * c:(j + 1) * c, h * c:(h + 1) * c] = p.astype(BF16)
            new.append(sh)
        acc[...] = acc[...] + _dot(ckvt[s], pbuf[...])
        return jnp.concatenate(new, axis=1)

    ssum = lax.fori_loop(0, nsc, pv_super, jnp.zeros((SUBLANES, nh * c), F32))
    denom = jnp.sum(ssum, axis=0, keepdims=True)

    for h in range(nh):
        cols = slice(h * c, (h + 1) * c)
        olat_t = (acc[:, cols] / denom[:, cols]).astype(BF16)
        ob = _dot(wuvt_ref[h], olat_t).T
        zb = zb_ref[:, h * DSA_DH:(h + 1) * DSA_DH]
        o_ref[:, h * DSA_DH:(h + 1) * DSA_DH] = (ob * _silu(zb)).astype(o_ref.dtype)


def _dsa(pa, pb, pc, kv_norm_g, wuk_t, wuv_t, bias_tiles, bsz, seq):
    c = LANES
    nb = seq // c
    nsc = nb // DSA_SUPER
    rows_all = DSA_HEADS * c
    za_off = (4 * GDN_W + 2 * D_MODEL) // DSA_W
    qi_off = DSA_W // (IDX_HEADS * IDX_DIM)
    kil_off = (DSA_W + IDX_HEADS * IDX_DIM) // c
    row = lambda b, i: b * nb + i
    return pl.pallas_call(
        _dsa_kernel,
        grid=(bsz, nb),
        in_specs=[
            pl.BlockSpec((c, DSA_W), lambda b, i: (row(b, i), 0)),
            pl.BlockSpec((c, IDX_HEADS * IDX_DIM), lambda b, i: (row(b, i), qi_off)),
            pl.BlockSpec((c, c), lambda b, i: (row(b, i), 2)),
            pl.BlockSpec((c, DSA_W), lambda b, i: (row(b, i), za_off)),
            pl.BlockSpec((seq, c), lambda b, i: (b, kil_off)),
            pl.BlockSpec((seq, c), lambda b, i: (b, kil_off + 1)),
            pl.BlockSpec((seq, KV_RANK), lambda b, i: (b, 0)),
            pl.BlockSpec((1, KV_RANK), lambda b, i: (0, 0)),
            pl.BlockSpec((DSA_HEADS, KV_RANK, DSA_DH), lambda b, i: (0, 0, 0)),
            pl.BlockSpec((DSA_HEADS, DSA_DH, KV_RANK), lambda b, i: (0, 0, 0)),
            pl.BlockSpec((_BIAS_TILES, DSA_HEADS, c, c), lambda b, i: (0, 0, 0, 0)),
        ],
        out_specs=pl.BlockSpec((c, DSA_W), lambda b, i: (row(b, i), 0)),
        out_shape=jax.ShapeDtypeStruct((bsz * seq, DSA_W), BF16),
        scratch_shapes=[pltpu.VMEM((seq, KV_RANK), BF16),
                        pltpu.VMEM((nsc, KV_RANK, DSA_SUPER * c), BF16),
                        pltpu.VMEM((nb, c, c), jnp.int32),
                        pltpu.VMEM((nb, c, c), F32),
                        pltpu.VMEM((rows_all, KV_RANK), BF16),
                        pltpu.VMEM((nsc, DSA_SUPER * c, rows_all), F32),
                        pltpu.VMEM((DSA_SUPER * c, rows_all), BF16),
                        pltpu.VMEM((KV_RANK, rows_all), F32)],
        compiler_params=pltpu.CompilerParams(vmem_limit_bytes=VMEM_LIMIT),
        name="dsa_attention",
    )(pc, pc, pb, pa, pc, pc, pb, kv_norm_g.reshape(1, KV_RANK), wuk_t, wuv_t,
      bias_tiles)


def _out_kernel(oa_ref, ob_ref, gl_ref, x_ref, gate_ref, woa_ref, wob_ref, wout_ref,
                fg_ref, o_ref):
    d = x_ref.shape[1]
    ya = _dot(oa_ref[...], woa_ref[...])
    yb = _dot(ob_ref[...], wob_ref[...])
    gl = gl_ref[...]
    mix = _sigmoid(gl[:, :d]) * ya + _sigmoid(gl[:, d:]) * yb
    out = _dot(mix.astype(BF16), wout_ref[...])
    xn = x_ref[...] + gate_ref[0] * out
    y = xn * lax.rsqrt(jnp.mean(xn * xn, axis=-1, keepdims=True) + EPS)
    o_ref[...] = y * fg_ref[...]


def _merge_out(oa, ob, pa, x2, gate, w_o_gdn, w_o_dsa, w_out, final_g, bsz, seq):
    d = D_MODEL
    tm = 512
    nt = seq // tm
    gl_off = (4 * GDN_W) // (2 * d)
    row = lambda b, i: (b * nt + i, 0)
    full = lambda b, i: (0, 0)
    return pl.pallas_call(
        _out_kernel,
        grid=(bsz, nt),
        in_specs=[pl.BlockSpec((tm, GDN_W), row),
                  pl.BlockSpec((tm, DSA_W), row),
                  pl.BlockSpec((tm, 2 * d), lambda b, i: (b * nt + i, gl_off)),
                  pl.BlockSpec((tm, d), row),
                  pl.BlockSpec((1, 1, d), lambda b, i: (b, 0, 0)),
                  pl.BlockSpec((GDN_W, d), full),
                  pl.BlockSpec((DSA_W, d), full),
                  pl.BlockSpec((d, d), full),
                  pl.BlockSpec((1, d), full)],
        out_specs=pl.BlockSpec((tm, d), row),
        out_shape=jax.ShapeDtypeStruct((bsz * seq, d), F32),
        compiler_params=pltpu.CompilerParams(vmem_limit_bytes=VMEM_LIMIT),
        name="merge_out",
    )(oa, ob, pa, x2, gate.reshape(bsz, 1, d), w_o_gdn, w_o_dsa, w_out,
      final_g.reshape(1, d))


def _split_w_in(w):
    o = 0
    names = ("qa", "ka", "va", "za", "aa", "ba", "qb", "ckv", "zb", "qi", "ki", "wi", "gl")
    sizes = (GDN_HEADS * GDN_DK, GDN_HEADS * GDN_DK, GDN_W, GDN_W, GDN_HEADS, GDN_HEADS,
             DSA_W, KV_RANK, DSA_W, IDX_HEADS * IDX_DIM, IDX_DIM, IDX_HEADS, 2 * D_MODEL)
    col = {}
    for nm, sz in zip(names, sizes):
        col[nm] = w[:, o:o + sz]
        o += sz
    k = w.shape[0]
    z = lambda n: jnp.zeros((k, n), w.dtype)
    wa = jnp.concatenate([col["qa"], col["ka"], col["va"], col["za"], col["gl"], col["zb"]], 1)
    n_misc = 2 * GDN_HEADS + IDX_HEADS
    wb = jnp.concatenate([col["ckv"], col["aa"], col["ba"], col["wi"], z(LANES - n_misc)], 1)
    pad = z(LANES - IDX_DIM)
    wc = jnp.concatenate([col["qb"], col["qi"], col["ki"], pad, pad, col["ki"]], 1)
    return wa.astype(BF16), wb.astype(BF16), wc.astype(BF16)


def kernel(x, c, norm_g, w_ada, b_ada, w_in, conv_w, a_log, dt_bias, gdn_norm_g, w_o_gdn,
           kv_norm_g, w_uk, w_uv, w_o_dsa, w_out, rel_bias, final_g):
    bsz, seq, d = x.shape
    depth = norm_g.shape[0]
    assert depth == 1, "the final RMSNorm is fused into the layer's merge kernel"
    x2 = x.reshape(bsz * seq, d)
    bias_tiles = _bias_tiles(rel_bias)
    for l in range(depth):
        mod = _adaln_mod(c, w_ada[l], b_ada[l])
        shift, scale, gate = mod[:, :d], mod[:, d:2 * d], mod[:, 2 * d:]
        h = _modulated_norm(x2, norm_g[l], scale, shift, seq)
        wa, wb, wc = _split_w_in(w_in[l])
        pa = _project(h, wa, F32, 512, tm=2048)
        pb = _project(h, wb, F32, wb.shape[1], tm=2048)
        pc = _project(h, wc, BF16, wc.shape[1] // 2, tm=2048)
        oa = _gdn(pa, pb, conv_w[l], a_log[l], dt_bias[l], gdn_norm_g[l], bsz, seq)
        wuk_t = jnp.transpose(w_uk[l], (1, 0, 2)).astype(BF16)
        wuv_t = jnp.transpose(w_uv[l], (1, 2, 0)).astype(BF16)
        ob = _dsa(pa, pb, pc, kv_norm_g[l], wuk_t, wuv_t, bias_tiles, bsz, seq)
        x2 = _merge_out(oa, ob, pa, x2, gate, w_o_gdn[l].astype(BF16),
                        w_o_dsa[l].astype(BF16), w_out[l].astype(BF16), final_g,
                        bsz, seq)
    return x2.reshape(bsz, seq, d)
```

```python
import math

import numpy as np
import jax
import jax.numpy as jnp
from jax import lax
from jax.experimental import pallas as pl
from jax.experimental.pallas import tpu as pltpu

D_MODEL = 1024
GDN_HEADS = 8
GDN_DK = 128
GDN_DV = 128
CONV_K = 4
DSA_HEADS = 8
DSA_DH = 128
KV_RANK = 256
IDX_HEADS = 8
IDX_DIM = 64
TOPK_MAX = 256
NUM_BUCKETS = 32
MAX_DISTANCE = 128
EPS = 1e-6

LANES = 128
SUBLANES = 8
GDN_GROUP = 8
DSA_SUPER = 4
ONES_ROWS = 16
GDN_W = GDN_HEADS * GDN_DV
DSA_W = DSA_HEADS * DSA_DH
LOG2E = math.log2(math.e)

F32 = jnp.float32
BF16 = jnp.bfloat16
INT_MIN = -2 ** 31
VMEM_LIMIT = 48 * 1024 * 1024

_NT = (((1,), (1,)), ((), ()))


def _dot(a, b):
    return jnp.dot(a, b, preferred_element_type=F32)


def _dot_nt(a, b):
    return lax.dot_general(a, b, _NT, preferred_element_type=F32)


def _split2(a):
    hi = a.astype(BF16)
    lo = (a - hi.astype(F32)).astype(BF16)
    return hi, lo


def _mm3(a, b):
    ah, al = _split2(a)
    bh, bl = _split2(b)
    n = b.shape[1]
    big = _dot(ah, jnp.concatenate([bh, bl], axis=1))
    return big[:, :n] + (big[:, n:] + _dot(al, bh))


def _sigmoid(x):
    return 0.5 * jnp.tanh(0.5 * x) + 0.5


def _silu(x):
    h = 0.5 * x
    return h * jnp.tanh(h) + h


def _softplus(x):
    return jnp.maximum(x, 0.0) + jnp.log1p(jnp.exp(-jnp.abs(x)))


def _fold_rows(x, op):
    parts = [x[r:r + SUBLANES, :] for r in range(0, x.shape[0], SUBLANES)]
    while len(parts) > 1:
        nxt = [op(parts[j], parts[j + 1]) for j in range(0, len(parts) - 1, 2)]
        if len(parts) % 2:
            nxt.append(parts[-1])
        parts = nxt
    return parts[0]


def _mod_kernel(c_ref, w_ref, b_ref, o_ref):
    a = _silu(c_ref[...])
    o_ref[...] = _mm3(a, w_ref[...]) + b_ref[...]


def _adaln_mod(c, w_ada, b_ada):
    bsz, d = c.shape
    n = w_ada.shape[1]
    tn = 512
    return pl.pallas_call(
        _mod_kernel,
        grid=(n // tn,),
        in_specs=[pl.BlockSpec((bsz, d), lambda j: (0, 0)),
                  pl.BlockSpec((d, tn), lambda j: (0, j)),
                  pl.BlockSpec((1, tn), lambda j: (0, j))],
        out_specs=pl.BlockSpec((bsz, tn), lambda j: (0, j)),
        out_shape=jax.ShapeDtypeStruct((bsz, n), F32),
        name="adaln_mod",
    )(c, w_ada, b_ada.reshape(1, n))


def _h_kernel(x_ref, g_ref, sc_ref, sh_ref, o_ref):
    x = x_ref[...]
    y = x * lax.rsqrt(jnp.mean(x * x, axis=-1, keepdims=True) + EPS)
    y = y * g_ref[...]
    o_ref[...] = (y * (1.0 + sc_ref[0]) + sh_ref[0]).astype(o_ref.dtype)


def _modulated_norm(x2, norm_g, scale, shift, seq):
    m, d = x2.shape
    bsz = m // seq
    tm = 512
    nt = seq // tm
    return pl.pallas_call(
        _h_kernel,
        grid=(bsz, nt),
        in_specs=[pl.BlockSpec((tm, d), lambda b, i: (b * nt + i, 0)),
                  pl.BlockSpec((1, d), lambda b, i: (0, 0)),
                  pl.BlockSpec((1, 1, d), lambda b, i: (b, 0, 0)),
                  pl.BlockSpec((1, 1, d), lambda b, i: (b, 0, 0))],
        out_specs=pl.BlockSpec((tm, d), lambda b, i: (b * nt + i, 0)),
        out_shape=jax.ShapeDtypeStruct((m, d), BF16),
        name="modulated_norm",
    )(x2, norm_g.reshape(1, d), scale.reshape(bsz, 1, d), shift.reshape(bsz, 1, d))


def _proj_kernel(h_ref, w_ref, o_ref):
    o_ref[...] = _dot(h_ref[...], w_ref[...]).astype(o_ref.dtype)


def _project(h, w, out_dtype, tn, tm=1024):
    m, k = h.shape
    n = w.shape[1]
    return pl.pallas_call(
        _proj_kernel,
        grid=(m // tm, n // tn),
        in_specs=[pl.BlockSpec((tm, k), lambda i, j: (i, 0)),
                  pl.BlockSpec((k, tn), lambda i, j: (0, j))],
        out_specs=pl.BlockSpec((tm, tn), lambda i, j: (i, j)),
        out_shape=jax.ShapeDtypeStruct((m, n), out_dtype),
        compiler_params=pltpu.CompilerParams(vmem_limit_bytes=VMEM_LIMIT),
        name="in_proj",
    )(h, w)


def _unit_lower_inverses(a_list):
    n = a_list[0].shape[0]
    ri = lax.broadcasted_iota(jnp.int32, (n, n), 0)
    ci = lax.broadcasted_iota(jnp.int32, (n, n), 1)
    x = ri ^ ci
    eye = (ri == ci).astype(F32)
    ts = [eye - jnp.where(x == 1, a, 0.0) for a in a_list]
    s = 2
    while s < n:
        level = (x >= s) & (x < 2 * s)
        tb = [t.astype(BF16) for t in ts]
        ps = [_dot(jnp.where(level, a, 0.0).astype(BF16), t) for a, t in zip(a_list, tb)]
        ts = [t - _dot(t16, p.astype(BF16)) for t, t16, p in zip(ts, tb, ps)]
        s *= 2
    return ts


def _refined_solves(t_list, a_list, rhs_list):
    tb = [t.astype(BF16) for t in t_list]
    x0 = [_dot(t, r.astype(BF16)) for t, r in zip(tb, rhs_list)]
    res = [r - (x + _mm3(a, x)) for r, x, a in zip(rhs_list, x0, a_list)]
    return [x + _dot(t, e.astype(BF16)) for x, t, e in zip(x0, tb, res)]


def _gdn_kernel(q_ref, k_ref, v_ref, misc_ref, cwq_ref, cwk_ref, cwv_ref,
                alog_ref, dtb_ref, z_ref, ng_ref, o_ref, qs, ks, vs, gs, bs,
                ous, cus, qws, nws, edl):
    s = pl.program_id(0)
    n_items = pl.num_programs(0) - 1
    cur = (s + 1) % 2
    nxt = s % 2
    hd = jnp.minimum(s, n_items - 1) % GDN_HEADS
    seq = q_ref.shape[0]
    c = LANES
    nchunk = seq // c
    row8 = lax.broadcasted_iota(jnp.int32, (SUBLANES, c), 0)
    lane = lax.broadcasted_iota(jnp.int32, (c, c), 1)
    ri = lax.broadcasted_iota(jnp.int32, (c, c), 0)
    ci = lane
    tri = (ri >= ci).astype(BF16)

    @pl.when(s == 0)
    def _():
        for ref in (qs, ks, vs, gs, bs):
            ref[cur] = jnp.zeros((seq, c), F32)

    def prologue(n):
        sl = pl.ds(pl.multiple_of(n * c, c), c)
        prev = pl.ds(pl.multiple_of(jnp.maximum(n * c - SUBLANES, 0), SUBLANES), SUBLANES)

        def conv_silu(u_ref, w_ref):
            u = u_ref[sl, :]
            w = w_ref[...]
            tail = jnp.where(n > 0, u_ref[prev, :], 0.0)
            y = u * w[CONV_K - 1:CONV_K, :]
            for k in range(1, CONV_K):
                sh = pltpu.roll(u, k, 0)
                head = jnp.where(row8 < k, pltpu.roll(tail, k, 0), sh[:SUBLANES, :])
                sh = jnp.concatenate([head, sh[SUBLANES:, :]], axis=0)
                y = y + sh * w[CONV_K - 1 - k:CONV_K - k, :]
            return _silu(y)

        def l2n(x):
            return x * lax.rsqrt(jnp.sum(x * x, axis=-1, keepdims=True) + EPS)

        qs[nxt, sl, :] = l2n(conv_silu(q_ref, cwq_ref)) * (GDN_DK ** -0.5)
        ks[nxt, sl, :] = l2n(conv_silu(k_ref, cwk_ref))
        vs[nxt, sl, :] = conv_silu(v_ref, cwv_ref)
        misc = misc_ref[sl, :]
        aa = jnp.sum(jnp.where(lane == hd, misc, 0.0), axis=-1, keepdims=True)
        ba = jnp.sum(jnp.where(lane == hd + GDN_HEADS, misc, 0.0), axis=-1, keepdims=True)
        gs[nxt, sl, :] = -jnp.exp(alog_ref[0]) * _softplus(aa + dtb_ref[0])
        bs[nxt, sl, :] = jnp.broadcast_to(_sigmoid(ba), (c, c))

    def cumdecay(g):
        g1 = g.astype(BF16)
        r1 = g - g1.astype(F32)
        g2 = r1.astype(BF16)
        g3 = (r1 - g2.astype(F32)).astype(BF16)
        d = _dot(tri, jnp.concatenate([g1, g2, g3], axis=1))
        return d[:, :c] + (d[:, c:2 * c] + d[:, 2 * c:])

    n_groups = nchunk // GDN_GROUP
    pro_in_prepare = nchunk // (2 * n_groups)

    def prepare_group(gi, carry):
        ns = [gi * GDN_GROUP + j for j in range(GDN_GROUP)]
        sls = [pl.ds(pl.multiple_of(n * c, c), c) for n in ns]
        dcols = [cumdecay(gs[cur, sl, :]) for sl in sls]
        ks_ = [ks[cur, sl, :] for sl in sls]
        kbs = [k * bs[cur, sl, :] for k, sl in zip(ks_, sls)]
        kq = [_dot_nt(jnp.concatenate([kb, qs[cur, sl, :]], axis=0).astype(BF16),
                      k.astype(BF16))
              for kb, k, sl in zip(kbs, ks_, sls)]
        gammas = [jnp.exp(jnp.where(ri >= ci, d - d.T, -jnp.inf)) for d in dcols]
        a_list = [jnp.where(ri > ci, m[:c] * gm, 0.0) for m, gm in zip(kq, gammas)]
        attns = [(m[c:] * gm).astype(BF16) for m, gm in zip(kq, gammas)]
        ecols = [jnp.exp(d) for d in dcols]
        rhs = [jnp.concatenate([kb * e, vs[cur, sl, :] * bs[cur, sl, :]], axis=1)
               for sl, kb, e in zip(sls, kbs, ecols)]
        tinvs = _unit_lower_inverses(a_list)
        wu = [x.astype(BF16) for x in _refined_solves(tinvs, a_list, rhs)]
        kdts = [(k * jnp.exp(d[c - 1:c, :] - d)).T.astype(BF16) for k, d in zip(ks_, dcols)]
        aw = [_dot(at, x) for at, x in zip(attns, wu)]
        kw = [_dot(kd, x) for kd, x in zip(kdts, wu)]
        for n, sl, d, e, a_, k_ in zip(ns, sls, dcols, ecols, aw, kw):
            qws[sl, :] = (qs[cur, sl, :] * e - a_[:, :c]).astype(BF16)
            ous[sl, :] = a_[:, c:]
            nws[sl, :] = k_[:, :c].astype(BF16)
            cus[sl, :] = k_[:, c:]
            edl[n] = jnp.broadcast_to(jnp.exp(d[c - 1:c, :]), (SUBLANES, c))
        for j in range(pro_in_prepare):
            prologue(gi * pro_in_prepare + j)
        return carry

    lax.fori_loop(0, n_groups, prepare_group, 0)

    def recur_pair(it, state):
        for j in range(2):
            n = 2 * it + j
            sl = pl.ds(pl.multiple_of(n * c, c), c)
            sb = state.astype(BF16)
            ous[sl, :] = _dot(qws[sl, :], sb) + ous[sl, :]
            state = state * edl[n][0:1, :] + (cus[sl, :] - _dot(nws[sl, :], sb))
        prologue(nchunk // 2 + it)
        return state

    lax.fori_loop(0, nchunk // 2, recur_pair, jnp.zeros((GDN_DK, GDN_DV), F32))

    o = ous[...]
    on = o * lax.rsqrt(jnp.mean(o * o, axis=-1, keepdims=True) + EPS) * ng_ref[...]
    o_ref[...] = (on * _silu(z_ref[...])).astype(o_ref.dtype)


def _gdn(pa, pb, conv_w, a_log, dt_bias, gdn_norm_g, bsz, seq):
    nh = GDN_HEADS
    c = LANES
    n_items = bsz * nh
    assert (seq // c) % (2 * GDN_GROUP) == 0
    alog_b = jnp.broadcast_to(a_log.reshape(nh, 1, 1), (nh, 1, c))
    dtb_b = jnp.broadcast_to(dt_bias.reshape(nh, 1, 1), (nh, 1, c))
    nxt = lambda s: jnp.minimum(s, n_items - 1)
    cur = lambda s: jnp.maximum(s - 1, 0)
    blk = lambda off: pl.BlockSpec((seq, c), lambda s: (nxt(s) // nh, off + nxt(s) % nh))
    cw = lambda off: pl.BlockSpec((CONV_K, c), lambda s: (0, off + nxt(s) % nh))
    hv = pl.BlockSpec((1, 1, c), lambda s: (nxt(s) % nh, 0, 0))
    return pl.pallas_call(
        _gdn_kernel,
        grid=(n_items + 1,),
        in_specs=[blk(0), blk(nh), blk(2 * nh),
                  pl.BlockSpec((seq, c), lambda s: (nxt(s) // nh, 2)),
                  cw(0), cw(nh), cw(2 * nh), hv, hv,
                  pl.BlockSpec((seq, c), lambda s: (cur(s) // nh, 3 * nh + cur(s) % nh)),
                  pl.BlockSpec((1, c), lambda s: (0, 0))],
        out_specs=pl.BlockSpec((seq, c), lambda s: (cur(s) // nh, cur(s) % nh)),
        out_shape=jax.ShapeDtypeStruct((bsz * seq, GDN_W), BF16),
        scratch_shapes=([pltpu.VMEM((2, seq, c), F32)] * 5
                        + [pltpu.VMEM((seq, c), F32)] * 2
                        + [pltpu.VMEM((seq, c), BF16)] * 2
                        + [pltpu.VMEM((seq // c, SUBLANES, c), F32)]),
        compiler_params=pltpu.CompilerParams(vmem_limit_bytes=VMEM_LIMIT),
        name="gated_deltanet",
    )(pa, pa, pa, pb, conv_w, conv_w, conv_w, alog_b, dtb_b, pa,
      gdn_norm_g.reshape(1, c))


def _t5_bucket_table(n_tiles):
    s = np.arange(LANES)[:, None]
    q = np.arange(LANES)[None, :]
    out = []
    max_exact = NUM_BUCKETS // 2
    for d in range(n_tiles):
        n = np.maximum(d * LANES + q - s, 0)
        nf = np.maximum(n, 1).astype(np.float32)
        large = max_exact + (np.log(nf / max_exact) / math.log(MAX_DISTANCE / max_exact)
                             * (NUM_BUCKETS - max_exact)).astype(np.int32)
        large = np.minimum(large, NUM_BUCKETS - 1)
        out.append(np.where(n < max_exact, n, large))
    return np.stack(out).astype(np.int32)


_BIAS_TILES = 3
assert (_BIAS_TILES - 1) * LANES - (LANES - 1) >= MAX_DISTANCE


def _bias_kernel(bkt_ref, rel_ref, o_ref):
    hd = pl.program_id(0)
    for d in range(_BIAS_TILES):
        bkt = bkt_ref[d]
        acc = jnp.zeros((LANES, LANES), F32)
        for j in range(NUM_BUCKETS):
            acc = jnp.where(bkt == j, rel_ref[j, hd] * LOG2E, acc)
        o_ref[d, 0] = acc


def _bias_tiles(rel_bias):
    bkt = jnp.asarray(_t5_bucket_table(_BIAS_TILES))
    return pl.pallas_call(
        _bias_kernel,
        grid=(DSA_HEADS,),
        in_specs=[pl.BlockSpec((_BIAS_TILES, LANES, LANES), lambda h: (0, 0, 0)),
                  pl.BlockSpec(memory_space=pltpu.SMEM)],
        out_specs=pl.BlockSpec((_BIAS_TILES, 1, LANES, LANES), lambda h: (0, h, 0, 0)),
        out_shape=jax.ShapeDtypeStruct((_BIAS_TILES, DSA_HEADS, LANES, LANES), F32),
        name="t5_bias_tiles",
    )(bkt, rel_bias)


def _dsa_kernel(qb_ref, qi_ref, misc_ref, zb_ref, kil_ref, kih_ref, ckv_ref,
                kvg_ref, wuk_ref, wuvt_ref, bias_ref, o_ref,
                ckvn, ckvt, keys, madd, qlat, lg, pbuf, acc):
    i = pl.program_id(1)
    c = LANES
    sc_w = DSA_SUPER * c
    nsc = i // DSA_SUPER + 1
    nh = DSA_HEADS
    seq = ckv_ref.shape[0]

    @pl.when(i == 0)
    def _():
        x = ckv_ref[...]
        y = x * lax.rsqrt(jnp.mean(x * x, axis=-1, keepdims=True) + EPS) * kvg_ref[...]
        ckvn[...] = y.astype(BF16)
        for r in range(seq // c):
            ckvt[r // DSA_SUPER, :KV_RANK, (r % DSA_SUPER) * c:(r % DSA_SUPER + 1) * c] = (
                y[r * c:(r + 1) * c, :].T.astype(BF16))
        ckvt[:, KV_RANK:, :] = jnp.ones((seq // sc_w, ONES_ROWS, sc_w), BF16)

    ri = lax.broadcasted_iota(jnp.int32, (c, c), 0)
    ci = lax.broadcasted_iota(jnp.int32, (c, c), 1)
    misc_t = misc_ref[...].T
    wscale = IDX_HEADS ** -0.5 * IDX_DIM ** -0.5
    wrow = [misc_t[2 * GDN_HEADS + h:2 * GDN_HEADS + h + 1, :] * wscale
            for h in range(IDX_HEADS)]
    npair = IDX_HEADS // 2
    qi4 = jnp.concatenate([qi_ref[:, p * c:(p + 1) * c] for p in range(npair)], axis=0)

    def score_super(s, carry):
        ccs = [s * DSA_SUPER + j for j in range(DSA_SUPER)]
        sls = [pl.ds(pl.multiple_of(cc * c, c), c) for cc in ccs]
        dots = [_dot_nt(jnp.concatenate([kil_ref[sl, :], kih_ref[sl, :]], axis=0), qi4)
                for sl in sls]
        for cc, d in zip(ccs, dots):
            acc_s = jnp.zeros((c, c), F32)
            for p in range(npair):
                acc_s = acc_s + wrow[2 * p] * jnp.maximum(d[:c, p * c:(p + 1) * c], 0.0)
                acc_s = acc_s + wrow[2 * p + 1] * jnp.maximum(d[c:, p * c:(p + 1) * c], 0.0)
            adm = (cc * c + ri) <= (i * c + ci)
            acc_s = jnp.where(adm, acc_s, -jnp.inf)
            bits = lax.bitcast_convert_type(acc_s, jnp.int32)
            keys[cc] = jnp.where(bits < 0, bits ^ 0x7FFFFFFF, bits)
        return carry

    lax.fori_loop(0, nsc, score_super, 0)

    kf = float(TOPK_MAX)

    def count(pred):
        def body(s, cnt):
            parts = [_fold_rows(jnp.where(pred(keys[s * DSA_SUPER + j]), 1.0, 0.0), jnp.add)
                     for j in range(DSA_SUPER)]
            return cnt + ((parts[0] + parts[1]) + (parts[2] + parts[3]))
        cnt = lax.fori_loop(0, nsc, body, jnp.zeros((SUBLANES, c), F32))
        return jnp.sum(cnt, axis=0, keepdims=True)

    def bit_step(it, carry):
        v, n_v = carry
        cand = v + jnp.left_shift(jnp.int32(1), 31 - it)
        tot = count(lambda kc: kc >= cand)
        take = tot >= kf
        return jnp.where(take, cand, v), jnp.where(take, tot, n_v)

    n_all = jnp.full((1, c), 1.0, F32) * (nsc * sc_w).astype(F32)
    thr, n_ge = lax.fori_loop(0, 32, bit_step,
                              (jnp.full((1, c), INT_MIN, jnp.int32), n_all))
    n_gt = count(lambda kc: kc > thr)
    need = kf - n_gt
    any_ties = jnp.max(n_ge - n_gt) > 1.0
    lower = (ci < ri).astype(BF16)

    def mask_super(s, running):
        ccs = [s * DSA_SUPER + j for j in range(DSA_SUPER)]
        kcs = [keys[cc] for cc in ccs]
        eqs = [jnp.where(kc == thr, 1.0, 0.0) for kc in kcs]
        prefix = [_dot(lower, eq.astype(BF16)) for eq in eqs]
        totals = [jnp.sum(_fold_rows(eq, jnp.add), axis=0, keepdims=True) for eq in eqs]
        for cc, kc, eq, pre, tot in zip(ccs, kcs, eqs, prefix, totals):
            sel = jnp.where(kc > thr, 1.0, jnp.where(running + pre < need, eq, 0.0))
            adm = (cc * c + ri) <= (i * c + ci)
            madd[cc] = jnp.where(adm & (sel > 0.5), 0.0, -jnp.inf)
            running = running + tot
        return running

    @pl.when(any_ties)
    def _():
        lax.fori_loop(0, nsc, mask_super, jnp.zeros((1, c), F32))

    @pl.when(jnp.logical_not(any_ties))
    def _():
        def mask_chunk(cc, carry):
            adm = (cc * c + ri) <= (i * c + ci)
            madd[cc] = jnp.where(adm & (keys[cc] >= thr), 0.0, -jnp.inf)
            return carry
        lax.fori_loop(0, nsc * DSA_SUPER, mask_chunk, 0)

    for h in range(nh):
        qh = qb_ref[:, h * DSA_DH:(h + 1) * DSA_DH]
        qlat[h * c:(h + 1) * c, :] = (
            _dot_nt(qh, wuk_ref[h]) * (DSA_DH ** -0.5 * LOG2E)).astype(BF16)

    acc[...] = jnp.zeros(acc.shape, F32)

    def logit_super(s, mx):
        kv = ckvn[pl.ds(pl.multiple_of(s * sc_w, sc_w), sc_w), :]
        l_all = _dot_nt(kv, qlat[...])
        new = []
        for h in range(nh):
            mh = mx[:, h * c:(h + 1) * c]
            for j in range(DSA_SUPER):
                cc = s * DSA_SUPER + j
                bidx = jnp.clip(i - cc, 0, _BIAS_TILES - 1)
                l = (l_all[j * c:(j + 1) * c, h * c:(h + 1) * c]
                     + bias_ref[bidx, h] + madd[cc])
                lg[s, j * c:(j + 1) * c, h * c:(h + 1) * c] = l
                mh = jnp.maximum(mh, _fold_rows(l, jnp.maximum))
            new.append(mh)
        return jnp.concatenate(new, axis=1)

    mx = lax.fori_loop(0, nsc, logit_super,
                       jnp.full((SUBLANES, nh * c), -jnp.inf, F32))
    m = jnp.max(mx, axis=0, keepdims=True)

    def pv_super(s, carry):
        for h in range(nh):
            mh = m[:, h * c:(h + 1) * c]
            for j in range(DSA_SUPER):
                p = jnp.exp2(lg[s, j * c:(j + 1) * c, h * c:(h + 1) * c] - mh)
                pbuf[j * c:(j + 1) * c, h * c:(h + 1) * c] = p.astype(BF16)
        acc[...] = acc[...] + _dot(ckvt[s], pbuf[...])
        return carry

    lax.fori_loop(0, nsc, pv_super, 0)
    denom = acc[KV_RANK:KV_RANK + 1, :]

    for h in range(nh):
        cols = slice(h * c, (h + 1) * c)
        olat_t = (acc[:KV_RANK, cols] / denom[:, cols]).astype(BF16)
        ob = _dot(wuvt_ref[h], olat_t).T
        zb = zb_ref[:, h * DSA_DH:(h + 1) * DSA_DH]
        o_ref[:, h * DSA_DH:(h + 1) * DSA_DH] = (ob * _silu(zb)).astype(o_ref.dtype)


def _dsa(pa, pb, pc, kv_norm_g, wuk_t, wuv_t, bias_tiles, bsz, seq):
    c = LANES
    nb = seq // c
    nsc = nb // DSA_SUPER
    rows_all = DSA_HEADS * c
    za_off = (4 * GDN_W + 2 * D_MODEL) // DSA_W
    qi_off = DSA_W // (IDX_HEADS * IDX_DIM)
    kil_off = (DSA_W + IDX_HEADS * IDX_DIM) // c
    row = lambda b, i: b * nb + i
    return pl.pallas_call(
        _dsa_kernel,
        grid=(bsz, nb),
        in_specs=[
            pl.BlockSpec((c, DSA_W), lambda b, i: (row(b, i), 0)),
            pl.BlockSpec((c, IDX_HEADS * IDX_DIM), lambda b, i: (row(b, i), qi_off)),
            pl.BlockSpec((c, c), lambda b, i: (row(b, i), 2)),
            pl.BlockSpec((c, DSA_W), lambda b, i: (row(b, i), za_off)),
            pl.BlockSpec((seq, c), lambda b, i: (b, kil_off)),
            pl.BlockSpec((seq, c), lambda b, i: (b, kil_off + 1)),
            pl.BlockSpec((seq, KV_RANK), lambda b, i: (b, 0)),
            pl.BlockSpec((1, KV_RANK), lambda b, i: (0, 0)),
            pl.BlockSpec((DSA_HEADS, KV_RANK, DSA_DH), lambda b, i: (0, 0, 0)),
            pl.BlockSpec((DSA_HEADS, DSA_DH, KV_RANK), lambda b, i: (0, 0, 0)),
            pl.BlockSpec((_BIAS_TILES, DSA_HEADS, c, c), lambda b, i: (0, 0, 0, 0)),
        ],
        out_specs=pl.BlockSpec((c, DSA_W), lambda b, i: (row(b, i), 0)),
        out_shape=jax.ShapeDtypeStruct((bsz * seq, DSA_W), BF16),
        scratch_shapes=[pltpu.VMEM((seq, KV_RANK), BF16),
                        pltpu.VMEM((nsc, KV_RANK + ONES_ROWS, DSA_SUPER * c), BF16),
                        pltpu.VMEM((nb, c, c), jnp.int32),
                        pltpu.VMEM((nb, c, c), F32),
                        pltpu.VMEM((rows_all, KV_RANK), BF16),
                        pltpu.VMEM((nsc, DSA_SUPER * c, rows_all), F32),
                        pltpu.VMEM((DSA_SUPER * c, rows_all), BF16),
                        pltpu.VMEM((KV_RANK + ONES_ROWS, rows_all), F32)],
        compiler_params=pltpu.CompilerParams(vmem_limit_bytes=VMEM_LIMIT),
        name="dsa_attention",
    )(pc, pc, pb, pa, pc, pc, pb, kv_norm_g.reshape(1, KV_RANK), wuk_t, wuv_t,
      bias_tiles)


def _out_kernel(oa_ref, ob_ref, gl_ref, x_ref, gate_ref, woa_ref, wob_ref, wout_ref,
                fg_ref, o_ref):
    d = x_ref.shape[1]
    ya = _dot(oa_ref[...], woa_ref[...])
    yb = _dot(ob_ref[...], wob_ref[...])
    gl = gl_ref[...]
    mix = _sigmoid(gl[:, :d]) * ya + _sigmoid(gl[:, d:]) * yb
    out = _dot(mix.astype(BF16), wout_ref[...])
    xn = x_ref[...] + gate_ref[0] * out
    y = xn * lax.rsqrt(jnp.mean(xn * xn, axis=-1, keepdims=True) + EPS)
    o_ref[...] = y * fg_ref[...]


def _merge_out(oa, ob, pa, x2, gate, w_o_gdn, w_o_dsa, w_out, final_g, bsz, seq):
    d = D_MODEL
    tm = 512
    nt = seq // tm
    gl_off = (4 * GDN_W) // (2 * d)
    row = lambda b, i: (b * nt + i, 0)
    full = lambda b, i: (0, 0)
    return pl.pallas_call(
        _out_kernel,
        grid=(bsz, nt),
        in_specs=[pl.BlockSpec((tm, GDN_W), row),
                  pl.BlockSpec((tm, DSA_W), row),
                  pl.BlockSpec((tm, 2 * d), lambda b, i: (b * nt + i, gl_off)),
                  pl.BlockSpec((tm, d), row),
                  pl.BlockSpec((1, 1, d), lambda b, i: (b, 0, 0)),
                  pl.BlockSpec((GDN_W, d), full),
                  pl.BlockSpec((DSA_W, d), full),
                  pl.BlockSpec((d, d), full),
                  pl.BlockSpec((1, d), full)],
        out_specs=pl.BlockSpec((tm, d), row),
        out_shape=jax.ShapeDtypeStruct((bsz * seq, d), F32),
        compiler_params=pltpu.CompilerParams(vmem_limit_bytes=VMEM_LIMIT),
        name="merge_out",
    )(oa, ob, pa, x2, gate.reshape(bsz, 1, d), w_o_gdn, w_o_dsa, w_out,
      final_g.reshape(1, d))


def _split_w_in(w):
    o = 0
    names = ("qa", "ka", "va", "za", "aa", "ba", "qb", "ckv", "zb", "qi", "ki", "wi", "gl")
    sizes = (GDN_HEADS * GDN_DK, GDN_HEADS * GDN_DK, GDN_W, GDN_W, GDN_HEADS, GDN_HEADS,
             DSA_W, KV_RANK, DSA_W, IDX_HEADS * IDX_DIM, IDX_DIM, IDX_HEADS, 2 * D_MODEL)
    w = w.astype(BF16)
    col = {}
    for nm, sz in zip(names, sizes):
        col[nm] = w[:, o:o + sz]
        o += sz
    k = w.shape[0]
    z = lambda n: jnp.zeros((k, n), w.dtype)
    wa = jnp.concatenate([col["qa"], col["ka"], col["va"], col["za"], col["gl"], col["zb"]], 1)
    n_misc = 2 * GDN_HEADS + IDX_HEADS
    wb = jnp.concatenate([col["ckv"], col["aa"], col["ba"], col["wi"], z(LANES - n_misc)], 1)
    pad = z(LANES - IDX_DIM)
    wc = jnp.concatenate([col["qb"], col["qi"], col["ki"], pad, pad, col["ki"]], 1)
    return wa, wb, wc


def kernel(x, c, norm_g, w_ada, b_ada, w_in, conv_w, a_log, dt_bias, gdn_norm_g, w_o_gdn,
           kv_norm_g, w_uk, w_uv, w_o_dsa, w_out, rel_bias, final_g):
    bsz, seq, d = x.shape
    depth = norm_g.shape[0]
    assert depth == 1, "the final RMSNorm is fused into the layer's merge kernel"
    x2 = x.reshape(bsz * seq, d)
    bias_tiles = _bias_tiles(rel_bias)
    for l in range(depth):
        mod = _adaln_mod(c, w_ada[l], b_ada[l])
        shift, scale, gate = mod[:, :d], mod[:, d:2 * d], mod[:, 2 * d:]
        h = _modulated_norm(x2, norm_g[l], scale, shift, seq)
        wa, wb, wc = _split_w_in(w_in[l])
        pa = _project(h, wa, F32, 512, tm=2048)
        pb = _project(h, wb, F32, wb.shape[1], tm=2048)
        pc = _project(h, wc, BF16, wc.shape[1] // 2, tm=2048)
        oa = _gdn(pa, pb, conv_w[l], a_log[l], dt_bias[l], gdn_norm_g[l], bsz, seq)
        wuk_t = jnp.transpose(w_uk[l], (1, 0, 2)).astype(BF16)
        wuv_t = jnp.transpose(w_uv[l], (1, 2, 0)).astype(BF16)
        ob = _dsa(pa, pb, pc, kv_norm_g[l], wuk_t, wuv_t, bias_tiles, bsz, seq)
        x2 = _merge_out(oa, ob, pa, x2, gate, w_o_gdn[l].astype(BF16),
                        w_o_dsa[l].astype(BF16), w_out[l].astype(BF16), final_g,
                        bsz, seq)
    return x2.reshape(bsz, seq, d)
```

```python
import math

import numpy as np
import jax
import jax.numpy as jnp
from jax import lax
from jax.experimental import pallas as pl
from jax.experimental.pallas import tpu as pltpu

D_MODEL = 1024
GDN_HEADS = 8
GDN_DK = 128
GDN_DV = 128
CONV_K = 4
DSA_HEADS = 8
DSA_DH = 128
KV_RANK = 256
IDX_HEADS = 8
IDX_DIM = 64
TOPK_MAX = 256
NUM_BUCKETS = 32
MAX_DISTANCE = 128
EPS = 1e-6

LANES = 128
SUBLANES = 8
DSA_SUPER = 4
ONES_ROWS = 16
GDN_W = GDN_HEADS * GDN_DV
DSA_W = DSA_HEADS * DSA_DH
LOG2E = math.log2(math.e)

F32 = jnp.float32
BF16 = jnp.bfloat16
INT_MIN = -2 ** 31
VMEM_LIMIT = 48 * 1024 * 1024

_NT = (((1,), (1,)), ((), ()))


def _dot(a, b):
    return jnp.dot(a, b, preferred_element_type=F32)


def _dot_nt(a, b):
    return lax.dot_general(a, b, _NT, preferred_element_type=F32)


def _split2(a):
    hi = a.astype(BF16)
    lo = (a - hi.astype(F32)).astype(BF16)
    return hi, lo


def _mm3(a, b):
    ah, al = _split2(a)
    bh, bl = _split2(b)
    n = b.shape[1]
    big = _dot(ah, jnp.concatenate([bh, bl], axis=1))
    return big[:, :n] + (big[:, n:] + _dot(al, bh))


def _sigmoid(x):
    return 0.5 * jnp.tanh(0.5 * x) + 0.5


def _silu(x):
    h = 0.5 * x
    return h * jnp.tanh(h) + h


def _softplus(x):
    return jnp.maximum(x, 0.0) + jnp.log1p(jnp.exp(-jnp.abs(x)))


def _fold_rows(x, op):
    parts = [x[r:r + SUBLANES, :] for r in range(0, x.shape[0], SUBLANES)]
    while len(parts) > 1:
        nxt = [op(parts[j], parts[j + 1]) for j in range(0, len(parts) - 1, 2)]
        if len(parts) % 2:
            nxt.append(parts[-1])
        parts = nxt
    return parts[0]


def _mod_kernel(c_ref, w_ref, b_ref, o_ref):
    a = _silu(c_ref[...])
    o_ref[...] = _mm3(a, w_ref[...]) + b_ref[...]


def _adaln_mod(c, w_ada, b_ada):
    bsz, d = c.shape
    n = w_ada.shape[1]
    tn = 512
    return pl.pallas_call(
        _mod_kernel,
        grid=(n // tn,),
        in_specs=[pl.BlockSpec((bsz, d), lambda j: (0, 0)),
                  pl.BlockSpec((d, tn), lambda j: (0, j)),
                  pl.BlockSpec((1, tn), lambda j: (0, j))],
        out_specs=pl.BlockSpec((bsz, tn), lambda j: (0, j)),
        out_shape=jax.ShapeDtypeStruct((bsz, n), F32),
        name="adaln_mod",
    )(c, w_ada, b_ada.reshape(1, n))


def _h_kernel(x_ref, g_ref, sc_ref, sh_ref, o_ref):
    x = x_ref[...]
    y = x * lax.rsqrt(jnp.mean(x * x, axis=-1, keepdims=True) + EPS)
    y = y * g_ref[...]
    o_ref[...] = (y * (1.0 + sc_ref[0]) + sh_ref[0]).astype(o_ref.dtype)


def _modulated_norm(x2, norm_g, scale, shift, seq):
    m, d = x2.shape
    bsz = m // seq
    tm = 512
    nt = seq // tm
    return pl.pallas_call(
        _h_kernel,
        grid=(bsz, nt),
        in_specs=[pl.BlockSpec((tm, d), lambda b, i: (b * nt + i, 0)),
                  pl.BlockSpec((1, d), lambda b, i: (0, 0)),
                  pl.BlockSpec((1, 1, d), lambda b, i: (b, 0, 0)),
                  pl.BlockSpec((1, 1, d), lambda b, i: (b, 0, 0))],
        out_specs=pl.BlockSpec((tm, d), lambda b, i: (b * nt + i, 0)),
        out_shape=jax.ShapeDtypeStruct((m, d), BF16),
        name="modulated_norm",
    )(x2, norm_g.reshape(1, d), scale.reshape(bsz, 1, d), shift.reshape(bsz, 1, d))


def _proj_kernel(h_ref, w_ref, o_ref):
    o_ref[...] = _dot(h_ref[...], w_ref[...]).astype(o_ref.dtype)


def _project(h, w, out_dtype, tn, tm=1024):
    m, k = h.shape
    n = w.shape[1]
    return pl.pallas_call(
        _proj_kernel,
        grid=(m // tm, n // tn),
        in_specs=[pl.BlockSpec((tm, k), lambda i, j: (i, 0)),
                  pl.BlockSpec((k, tn), lambda i, j: (0, j))],
        out_specs=pl.BlockSpec((tm, tn), lambda i, j: (i, j)),
        out_shape=jax.ShapeDtypeStruct((m, n), out_dtype),
        compiler_params=pltpu.CompilerParams(vmem_limit_bytes=VMEM_LIMIT),
        name="in_proj",
    )(h, w)


def _unit_lower_inverses(a_list, between=None):
    n = a_list[0].shape[0]
    ri = lax.broadcasted_iota(jnp.int32, (n, n), 0)
    ci = lax.broadcasted_iota(jnp.int32, (n, n), 1)
    x = ri ^ ci
    eye = (ri == ci).astype(F32)
    ts = [eye - jnp.where(x == 1, a, 0.0) for a in a_list]
    s = 2
    while s < n:
        level = (x >= s) & (x < 2 * s)
        tb = [t.astype(BF16) for t in ts]
        ps = [_dot(jnp.where(level, a, 0.0).astype(BF16), t) for a, t in zip(a_list, tb)]
        if between is not None:
            between()
        ts = [t - _dot(t16, p.astype(BF16)) for t, t16, p in zip(ts, tb, ps)]
        if between is not None:
            between()
        s *= 2
    return ts


def _gdn_kernel(q_ref, k_ref, v_ref, misc_ref, cwq_ref, cwk_ref, cwv_ref,
                alog_ref, dtb_ref, z_ref, ng_ref, o_ref, qs, ks, vs, ds, bs,
                qws, nws, cus, ous, edl):
    s = pl.program_id(0)
    n_items = pl.num_programs(0) - 2
    seq = q_ref.shape[0]
    c = LANES
    nch = seq // c
    wr = s % 2
    rd = (s + 1) % 2
    hd = jnp.minimum(s, n_items - 1) % GDN_HEADS
    row8 = lax.broadcasted_iota(jnp.int32, (SUBLANES, c), 0)
    ri = lax.broadcasted_iota(jnp.int32, (c, c), 0)
    ci = lax.broadcasted_iota(jnp.int32, (c, c), 1)
    tri = (ri >= ci).astype(BF16)
    ng = ng_ref[...]
    rows = [slice(n * c, (n + 1) * c) for n in range(nch)]

    @pl.when(s == 0)
    def _():
        for ref in (qs, ks, vs, ds, bs, cus, ous):
            ref[rd] = jnp.zeros((seq, c), F32)
        for ref in (qws, nws):
            ref[rd] = jnp.zeros((seq, c), BF16)
        edl[rd] = jnp.zeros((nch, SUBLANES, c), F32)

    def prologue(n):
        def conv_silu(u_ref, w_ref):
            u = u_ref[rows[n], :]
            w = w_ref[...]
            if n > 0:
                tail = u_ref[n * c - SUBLANES:n * c, :]
            else:
                tail = jnp.zeros((SUBLANES, c), F32)
            y = u * w[CONV_K - 1:CONV_K, :]
            for k in range(1, CONV_K):
                sh = pltpu.roll(u, k, 0)
                head = jnp.where(row8 < k, pltpu.roll(tail, k, 0), sh[:SUBLANES, :])
                sh = jnp.concatenate([head, sh[SUBLANES:, :]], axis=0)
                y = y + sh * w[CONV_K - 1 - k:CONV_K - k, :]
            return _silu(y)

        def l2n(x):
            return x * lax.rsqrt(jnp.sum(x * x, axis=-1, keepdims=True) + EPS)

        qs[wr, rows[n], :] = l2n(conv_silu(q_ref, cwq_ref)) * (GDN_DK ** -0.5)
        ks[wr, rows[n], :] = l2n(conv_silu(k_ref, cwk_ref))
        vs[wr, rows[n], :] = conv_silu(v_ref, cwv_ref)
        misc = misc_ref[rows[n], :]
        aa = jnp.sum(jnp.where(ci == hd, misc, 0.0), axis=-1, keepdims=True)
        ba = jnp.sum(jnp.where(ci == hd + GDN_HEADS, misc, 0.0), axis=-1, keepdims=True)
        bs[wr, rows[n], :] = jnp.broadcast_to(_sigmoid(ba), (c, c))
        g = -jnp.exp(alog_ref[0]) * _softplus(aa + dtb_ref[0])
        g1 = g.astype(BF16)
        r1 = g - g1.astype(F32)
        g2 = r1.astype(BF16)
        g3 = (r1 - g2.astype(F32)).astype(BF16)
        d = _dot(tri, jnp.concatenate([g1, g2, g3], axis=1))
        ds[wr, rows[n], :] = d[:, :c] + (d[:, c:2 * c] + d[:, 2 * c:])

    def recur(n, st):
        sb = st.astype(BF16)
        o = _dot(qws[rd, rows[n], :], sb) + ous[rd, rows[n], :]
        st = st * edl[rd, n][0:1, :] + (cus[rd, rows[n], :] - _dot(nws[rd, rows[n], :], sb))
        on = o * lax.rsqrt(jnp.mean(o * o, axis=-1, keepdims=True) + EPS) * ng
        o_ref[rows[n], :] = (on * _silu(z_ref[rows[n], :])).astype(o_ref.dtype)
        return st

    carry = [jnp.zeros((GDN_DK, GDN_DV), F32), 0]

    def hosted():
        n = carry[1]
        carry[0] = recur(n, carry[0])
        prologue(n)
        carry[1] = n + 1

    dcols = [ds[rd, r, :] for r in rows]
    ks_ = [ks[rd, r, :] for r in rows]
    kbs = [k * bs[rd, r, :] for k, r in zip(ks_, rows)]
    kq = [_dot_nt(jnp.concatenate([kb, qs[rd, r, :]], axis=0).astype(BF16), k.astype(BF16))
          for kb, k, r in zip(kbs, ks_, rows)]
    hosted()
    gammas = [jnp.exp(jnp.where(ri >= ci, d - d.T, -jnp.inf)) for d in dcols]
    a_list = [jnp.where(ri > ci, m[:c] * gm, 0.0) for m, gm in zip(kq, gammas)]
    attns = [(m[c:] * gm).astype(BF16) for m, gm in zip(kq, gammas)]
    ecols = [jnp.exp(d) for d in dcols]
    rhs = [jnp.concatenate([kb * e, vs[rd, r, :] * bs[rd, r, :]], axis=1)
           for r, kb, e in zip(rows, kbs, ecols)]
    hosted()
    tinvs = [t.astype(BF16) for t in _unit_lower_inverses(a_list, between=hosted)]
    x0 = [_dot(t, r.astype(BF16)) for t, r in zip(tinvs, rhs)]
    hosted()
    res = [r - (x + _mm3(a, x)) for r, x, a in zip(rhs, x0, a_list)]
    hosted()
    assert carry[1] == nch
    wu = [(x + _dot(t, e.astype(BF16))).astype(BF16) for x, t, e in zip(x0, tinvs, res)]
    kdts = [(k * jnp.exp(d[c - 1:c, :] - d)).T.astype(BF16) for k, d in zip(ks_, dcols)]
    aw = [_dot(at, x) for at, x in zip(attns, wu)]
    kw = [_dot(kd, x) for kd, x in zip(kdts, wu)]
    for n, (r, d, e, a_, k_) in enumerate(zip(rows, dcols, ecols, aw, kw)):
        qws[wr, r, :] = (qs[rd, r, :] * e - a_[:, :c]).astype(BF16)
        ous[wr, r, :] = a_[:, c:]
        nws[wr, r, :] = k_[:, :c].astype(BF16)
        cus[wr, r, :] = k_[:, c:]
        edl[wr, n] = jnp.broadcast_to(jnp.exp(d[c - 1:c, :]), (SUBLANES, c))


def _gdn(pa, pb, conv_w, a_log, dt_bias, gdn_norm_g, bsz, seq):
    nh = GDN_HEADS
    c = LANES
    nch = seq // c
    assert nch == 2 + 2 * (int(math.log2(c)) - 1) + 2
    n_items = bsz * nh
    alog_b = jnp.broadcast_to(a_log.reshape(nh, 1, 1), (nh, 1, c))
    dtb_b = jnp.broadcast_to(dt_bias.reshape(nh, 1, 1), (nh, 1, c))
    pro = lambda s: jnp.minimum(s, n_items - 1)
    rec = lambda s: jnp.clip(s - 2, 0, n_items - 1)
    blk = lambda off: pl.BlockSpec((seq, c), lambda s: (pro(s) // nh, off + pro(s) % nh))
    cw = lambda off: pl.BlockSpec((CONV_K, c), lambda s: (0, off + pro(s) % nh))
    hv = pl.BlockSpec((1, 1, c), lambda s: (pro(s) % nh, 0, 0))
    return pl.pallas_call(
        _gdn_kernel,
        grid=(n_items + 2,),
        in_specs=[blk(0), blk(nh), blk(2 * nh),
                  pl.BlockSpec((seq, c), lambda s: (pro(s) // nh, 2)),
                  cw(0), cw(nh), cw(2 * nh), hv, hv,
                  pl.BlockSpec((seq, c), lambda s: (rec(s) // nh, 3 * nh + rec(s) % nh)),
                  pl.BlockSpec((1, c), lambda s: (0, 0))],
        out_specs=pl.BlockSpec((seq, c), lambda s: (rec(s) // nh, rec(s) % nh)),
        out_shape=jax.ShapeDtypeStruct((bsz * seq, GDN_W), BF16),
        scratch_shapes=([pltpu.VMEM((2, seq, c), F32)] * 5
                        + [pltpu.VMEM((2, seq, c), BF16)] * 2
                        + [pltpu.VMEM((2, seq, c), F32)] * 2
                        + [pltpu.VMEM((2, nch, SUBLANES, c), F32)]),
        compiler_params=pltpu.CompilerParams(vmem_limit_bytes=VMEM_LIMIT),
        name="gated_deltanet",
    )(pa, pa, pa, pb, conv_w, conv_w, conv_w, alog_b, dtb_b, pa,
      gdn_norm_g.reshape(1, c))


def _t5_bucket_table(n_tiles):
    s = np.arange(LANES)[:, None]
    q = np.arange(LANES)[None, :]
    out = []
    max_exact = NUM_BUCKETS // 2
    for d in range(n_tiles):
        n = np.maximum(d * LANES + q - s, 0)
        nf = np.maximum(n, 1).astype(np.float32)
        large = max_exact + (np.log(nf / max_exact) / math.log(MAX_DISTANCE / max_exact)
                             * (NUM_BUCKETS - max_exact)).astype(np.int32)
        large = np.minimum(large, NUM_BUCKETS - 1)
        out.append(np.where(n < max_exact, n, large))
    return np.stack(out).astype(np.int32)


_BIAS_TILES = 3
assert (_BIAS_TILES - 1) * LANES - (LANES - 1) >= MAX_DISTANCE


def _bias_kernel(bkt_ref, rel_ref, o_ref):
    hd = pl.program_id(0)
    for d in range(_BIAS_TILES):
        bkt = bkt_ref[d]
        acc = jnp.zeros((LANES, LANES), F32)
        for j in range(NUM_BUCKETS):
            acc = jnp.where(bkt == j, rel_ref[j, hd] * LOG2E, acc)
        o_ref[d, 0] = acc


def _bias_tiles(rel_bias):
    bkt = jnp.asarray(_t5_bucket_table(_BIAS_TILES))
    return pl.pallas_call(
        _bias_kernel,
        grid=(DSA_HEADS,),
        in_specs=[pl.BlockSpec((_BIAS_TILES, LANES, LANES), lambda h: (0, 0, 0)),
                  pl.BlockSpec(memory_space=pltpu.SMEM)],
        out_specs=pl.BlockSpec((_BIAS_TILES, 1, LANES, LANES), lambda h: (0, h, 0, 0)),
        out_shape=jax.ShapeDtypeStruct((_BIAS_TILES, DSA_HEADS, LANES, LANES), F32),
        name="t5_bias_tiles",
    )(bkt, rel_bias)


def _dsa_kernel(qb_ref, qi_ref, misc_ref, zb_ref, kil_ref, kih_ref, ckv_ref,
                kvg_ref, wuk_ref, wuvt_ref, bias_ref, o_ref,
                ckvn, ckvt, keys, madd, qlat, lg, pbuf, acc):
    i = pl.program_id(1)
    c = LANES
    sc_w = DSA_SUPER * c
    nsc = i // DSA_SUPER + 1
    nh = DSA_HEADS
    seq = ckv_ref.shape[0]

    @pl.when(i == 0)
    def _():
        x = ckv_ref[...]
        y = x * lax.rsqrt(jnp.mean(x * x, axis=-1, keepdims=True) + EPS) * kvg_ref[...]
        ckvn[...] = y.astype(BF16)
        for r in range(seq // c):
            ckvt[r // DSA_SUPER, :KV_RANK, (r % DSA_SUPER) * c:(r % DSA_SUPER + 1) * c] = (
                y[r * c:(r + 1) * c, :].T.astype(BF16))
        ckvt[:, KV_RANK:, :] = jnp.ones((seq // sc_w, ONES_ROWS, sc_w), BF16)

    ri = lax.broadcasted_iota(jnp.int32, (c, c), 0)
    ci = lax.broadcasted_iota(jnp.int32, (c, c), 1)
    misc_t = misc_ref[...].T
    wscale = IDX_HEADS ** -0.5 * IDX_DIM ** -0.5
    wrow = [misc_t[2 * GDN_HEADS + h:2 * GDN_HEADS + h + 1, :] * wscale
            for h in range(IDX_HEADS)]
    npair = IDX_HEADS // 2
    qi4 = jnp.concatenate([qi_ref[:, p * c:(p + 1) * c] for p in range(npair)], axis=0)

    def score_super(s, carry):
        ccs = [s * DSA_SUPER + j for j in range(DSA_SUPER)]
        sls = [pl.ds(pl.multiple_of(cc * c, c), c) for cc in ccs]
        dots = [_dot_nt(jnp.concatenate([kil_ref[sl, :], kih_ref[sl, :]], axis=0), qi4)
                for sl in sls]
        for cc, d in zip(ccs, dots):
            acc_s = jnp.zeros((c, c), F32)
            for p in range(npair):
                acc_s = acc_s + wrow[2 * p] * jnp.maximum(d[:c, p * c:(p + 1) * c], 0.0)
                acc_s = acc_s + wrow[2 * p + 1] * jnp.maximum(d[c:, p * c:(p + 1) * c], 0.0)
            adm = (cc * c + ri) <= (i * c + ci)
            acc_s = jnp.where(adm, acc_s, -jnp.inf)
            bits = lax.bitcast_convert_type(acc_s, jnp.int32)
            keys[cc] = jnp.where(bits < 0, bits ^ 0x7FFFFFFF, bits)
        return carry

    lax.fori_loop(0, nsc, score_super, 0)

    kf = float(TOPK_MAX)

    def count(pred):
        def body(s, cnt):
            parts = [_fold_rows(jnp.where(pred(keys[s * DSA_SUPER + j]), 1.0, 0.0), jnp.add)
                     for j in range(DSA_SUPER)]
            return cnt + ((parts[0] + parts[1]) + (parts[2] + parts[3]))
        cnt = lax.fori_loop(0, nsc, body, jnp.zeros((SUBLANES, c), F32))
        return jnp.sum(cnt, axis=0, keepdims=True)

    def bit_step(it, carry):
        v, n_v = carry
        cand = v + jnp.left_shift(jnp.int32(1), 31 - it)
        tot = count(lambda kc: kc >= cand)
        take = tot >= kf
        return jnp.where(take, cand, v), jnp.where(take, tot, n_v)

    n_all = jnp.full((1, c), 1.0, F32) * (nsc * sc_w).astype(F32)
    thr, n_ge = lax.fori_loop(0, 32, bit_step,
                              (jnp.full((1, c), INT_MIN, jnp.int32), n_all))
    n_gt = count(lambda kc: kc > thr)
    need = kf - n_gt
    any_ties = jnp.max(n_ge - n_gt) > 1.0
    lower = (ci < ri).astype(BF16)

    def mask_super(s, running):
        ccs = [s * DSA_SUPER + j for j in range(DSA_SUPER)]
        kcs = [keys[cc] for cc in ccs]
        eqs = [jnp.where(kc == thr, 1.0, 0.0) for kc in kcs]
        prefix = [_dot(lower, eq.astype(BF16)) for eq in eqs]
        totals = [jnp.sum(_fold_rows(eq, jnp.add), axis=0, keepdims=True) for eq in eqs]
        for cc, kc, eq, pre, tot in zip(ccs, kcs, eqs, prefix, totals):
            sel = jnp.where(kc > thr, 1.0, jnp.where(running + pre < need, eq, 0.0))
            adm = (cc * c + ri) <= (i * c + ci)
            madd[cc] = jnp.where(adm & (sel > 0.5), 0.0, -jnp.inf)
            running = running + tot
        return running

    @pl.when(any_ties)
    def _():
        lax.fori_loop(0, nsc, mask_super, jnp.zeros((1, c), F32))

    @pl.when(jnp.logical_not(any_ties))
    def _():
        def mask_chunk(cc, carry):
            adm = (cc * c + ri) <= (i * c + ci)
            madd[cc] = jnp.where(adm & (keys[cc] >= thr), 0.0, -jnp.inf)
            return carry
        lax.fori_loop(0, nsc * DSA_SUPER, mask_chunk, 0)

    for h in range(nh):
        qh = qb_ref[:, h * DSA_DH:(h + 1) * DSA_DH]
        qlat[h * c:(h + 1) * c, :] = (
            _dot_nt(qh, wuk_ref[h]) * (DSA_DH ** -0.5 * LOG2E)).astype(BF16)

    acc[...] = jnp.zeros(acc.shape, F32)

    def logit_super(s, mx):
        kv = ckvn[pl.ds(pl.multiple_of(s * sc_w, sc_w), sc_w), :]
        l_all = _dot_nt(kv, qlat[...])
        new = []
        for h in range(nh):
            mh = mx[:, h * c:(h + 1) * c]
            for j in range(DSA_SUPER):
                cc = s * DSA_SUPER + j
                bidx = jnp.clip(i - cc, 0, _BIAS_TILES - 1)
                l = (l_all[j * c:(j + 1) * c, h * c:(h + 1) * c]
                     + bias_ref[bidx, h] + madd[cc])
                lg[s, j * c:(j + 1) * c, h * c:(h + 1) * c] = l
                mh = jnp.maximum(mh, _fold_rows(l, jnp.maximum))
            new.append(mh)
        return jnp.concatenate(new, axis=1)

    mx = lax.fori_loop(0, nsc, logit_super,
                       jnp.full((SUBLANES, nh * c), -jnp.inf, F32))
    m = jnp.max(mx, axis=0, keepdims=True)

    def pv_super(s, carry):
        for h in range(nh):
            mh = m[:, h * c:(h + 1) * c]
            for j in range(DSA_SUPER):
                p = jnp.exp2(lg[s, j * c:(j + 1) * c, h * c:(h + 1) * c] - mh)
                pbuf[j * c:(j + 1) * c, h * c:(h + 1) * c] = p.astype(BF16)
        acc[...] = acc[...] + _dot(ckvt[s], pbuf[...])
        return carry

    lax.fori_loop(0, nsc, pv_super, 0)
    denom = acc[KV_RANK:KV_RANK + 1, :]

    for h in range(nh):
        cols = slice(h * c, (h + 1) * c)
        olat_t = (acc[:KV_RANK, cols] / denom[:, cols]).astype(BF16)
        ob = _dot(wuvt_ref[h], olat_t).T
        zb = zb_ref[:, h * DSA_DH:(h + 1) * DSA_DH]
        o_ref[:, h * DSA_DH:(h + 1) * DSA_DH] = (ob * _silu(zb)).astype(o_ref.dtype)


def _dsa(pa, pb, pc, kv_norm_g, wuk_t, wuv_t, bias_tiles, bsz, seq):
    c = LANES
    nb = seq // c
    nsc = nb // DSA_SUPER
    rows_all = DSA_HEADS * c
    za_off = (4 * GDN_W + 2 * D_MODEL) // DSA_W
    qi_off = DSA_W // (IDX_HEADS * IDX_DIM)
    kil_off = (DSA_W + IDX_HEADS * IDX_DIM) // c
    row = lambda b, i: b * nb + i
    return pl.pallas_call(
        _dsa_kernel,
        grid=(bsz, nb),
        in_specs=[
            pl.BlockSpec((c, DSA_W), lambda b, i: (row(b, i), 0)),
            pl.BlockSpec((c, IDX_HEADS * IDX_DIM), lambda b, i: (row(b, i), qi_off)),
            pl.BlockSpec((c, c), lambda b, i: (row(b, i), 2)),
            pl.BlockSpec((c, DSA_W), lambda b, i: (row(b, i), za_off)),
            pl.BlockSpec((seq, c), lambda b, i: (b, kil_off)),
            pl.BlockSpec((seq, c), lambda b, i: (b, kil_off + 1)),
            pl.BlockSpec((seq, KV_RANK), lambda b, i: (b, 0)),
            pl.BlockSpec((1, KV_RANK), lambda b, i: (0, 0)),
            pl.BlockSpec((DSA_HEADS, KV_RANK, DSA_DH), lambda b, i: (0, 0, 0)),
            pl.BlockSpec((DSA_HEADS, DSA_DH, KV_RANK), lambda b, i: (0, 0, 0)),
            pl.BlockSpec((_BIAS_TILES, DSA_HEADS, c, c), lambda b, i: (0, 0, 0, 0)),
        ],
        out_specs=pl.BlockSpec((c, DSA_W), lambda b, i: (row(b, i), 0)),
        out_shape=jax.ShapeDtypeStruct((bsz * seq, DSA_W), BF16),
        scratch_shapes=[pltpu.VMEM((seq, KV_RANK), BF16),
                        pltpu.VMEM((nsc, KV_RANK + ONES_ROWS, DSA_SUPER * c), BF16),
                        pltpu.VMEM((nb, c, c), jnp.int32),
                        pltpu.VMEM((nb, c, c), F32),
                        pltpu.VMEM((rows_all, KV_RANK), BF16),
                        pltpu.VMEM((nsc, DSA_SUPER * c, rows_all), F32),
                        pltpu.VMEM((DSA_SUPER * c, rows_all), BF16),
                        pltpu.VMEM((KV_RANK + ONES_ROWS, rows_all), F32)],
        compiler_params=pltpu.CompilerParams(vmem_limit_bytes=VMEM_LIMIT),
        name="dsa_attention",
    )(pc, pc, pb, pa, pc, pc, pb, kv_norm_g.reshape(1, KV_RANK), wuk_t, wuv_t,
      bias_tiles)


def _out_kernel(oa_ref, ob_ref, gl_ref, x_ref, gate_ref, woa_ref, wob_ref, wout_ref,
                fg_ref, o_ref):
    d = x_ref.shape[1]
    ya = _dot(oa_ref[...], woa_ref[...])
    yb = _dot(ob_ref[...], wob_ref[...])
    gl = gl_ref[...]
    mix = _sigmoid(gl[:, :d]) * ya + _sigmoid(gl[:, d:]) * yb
    out = _dot(mix.astype(BF16), wout_ref[...])
    xn = x_ref[...] + gate_ref[0] * out
    y = xn * lax.rsqrt(jnp.mean(xn * xn, axis=-1, keepdims=True) + EPS)
    o_ref[...] = y * fg_ref[...]


def _merge_out(oa, ob, pa, x2, gate, w_o_gdn, w_o_dsa, w_out, final_g, bsz, seq):
    d = D_MODEL
    tm = 512
    nt = seq // tm
    gl_off = (4 * GDN_W) // (2 * d)
    row = lambda b, i: (b * nt + i, 0)
    full = lambda b, i: (0, 0)
    return pl.pallas_call(
        _out_kernel,
        grid=(bsz, nt),
        in_specs=[pl.BlockSpec((tm, GDN_W), row),
                  pl.BlockSpec((tm, DSA_W), row),
                  pl.BlockSpec((tm, 2 * d), lambda b, i: (b * nt + i, gl_off)),
                  pl.BlockSpec((tm, d), row),
                  pl.BlockSpec((1, 1, d), lambda b, i: (b, 0, 0)),
                  pl.BlockSpec((GDN_W, d), full),
                  pl.BlockSpec((DSA_W, d), full),
                  pl.BlockSpec((d, d), full),
                  pl.BlockSpec((1, d), full)],
        out_specs=pl.BlockSpec((tm, d), row),
        out_shape=jax.ShapeDtypeStruct((bsz * seq, d), F32),
        compiler_params=pltpu.CompilerParams(vmem_limit_bytes=VMEM_LIMIT),
        name="merge_out",
    )(oa, ob, pa, x2, gate.reshape(bsz, 1, d), w_o_gdn, w_o_dsa, w_out,
      final_g.reshape(1, d))


def _split_w_in(w):
    o = 0
    names = ("qa", "ka", "va", "za", "aa", "ba", "qb", "ckv", "zb", "qi", "ki", "wi", "gl")
    sizes = (GDN_HEADS * GDN_DK, GDN_HEADS * GDN_DK, GDN_W, GDN_W, GDN_HEADS, GDN_HEADS,
             DSA_W, KV_RANK, DSA_W, IDX_HEADS * IDX_DIM, IDX_DIM, IDX_HEADS, 2 * D_MODEL)
    w = w.astype(BF16)
    col = {}
    for nm, sz in zip(names, sizes):
        col[nm] = w[:, o:o + sz]
        o += sz
    k = w.shape[0]
    z = lambda n: jnp.zeros((k, n), w.dtype)
    wa = jnp.concatenate([col["qa"], col["ka"], col["va"], col["za"], col["gl"], col["zb"]], 1)
    n_misc = 2 * GDN_HEADS + IDX_HEADS
    wb = jnp.concatenate([col["ckv"], col["aa"], col["ba"], col["wi"], z(LANES - n_misc)], 1)
    pad = z(LANES - IDX_DIM)
    wc = jnp.concatenate([col["qb"], col["qi"], col["ki"], pad, pad, col["ki"]], 1)
    return wa, wb, wc


def kernel(x, c, norm_g, w_ada, b_ada, w_in, conv_w, a_log, dt_bias, gdn_norm_g, w_o_gdn,
           kv_norm_g, w_uk, w_uv, w_o_dsa, w_out, rel_bias, final_g):
    bsz, seq, d = x.shape
    depth = norm_g.shape[0]
    assert depth == 1, "the final RMSNorm is fused into the layer's merge kernel"
    x2 = x.reshape(bsz * seq, d)
    bias_tiles = _bias_tiles(rel_bias)
    for l in range(depth):
        mod = _adaln_mod(c, w_ada[l], b_ada[l])
        shift, scale, gate = mod[:, :d], mod[:, d:2 * d], mod[:, 2 * d:]
        h = _modulated_norm(x2, norm_g[l], scale, shift, seq)
        wa, wb, wc = _split_w_in(w_in[l])
        pa = _project(h, wa, F32, 512, tm=2048)
        pb = _project(h, wb, F32, wb.shape[1], tm=2048)
        pc = _project(h, wc, BF16, wc.shape[1] // 2, tm=2048)
        oa = _gdn(pa, pb, conv_w[l], a_log[l], dt_bias[l], gdn_norm_g[l], bsz, seq)
        wuk_t = jnp.transpose(w_uk[l], (1, 0, 2)).astype(BF16)
        wuv_t = jnp.transpose(w_uv[l], (1, 2, 0)).astype(BF16)
        ob = _dsa(pa, pb, pc, kv_norm_g[l], wuk_t, wuv_t, bias_tiles, bsz, seq)
        x2 = _merge_out(oa, ob, pa, x2, gate, w_o_gdn[l].astype(BF16),
                        w_o_dsa[l].astype(BF16), w_out[l].astype(BF16), final_g,
                        bsz, seq)
    return x2.reshape(bsz, seq, d)
```

```python
import math

import numpy as np
import jax
import jax.numpy as jnp
from jax import lax
from jax.experimental import pallas as pl
from jax.experimental.pallas import tpu as pltpu

D_MODEL = 1024
GDN_HEADS = 8
GDN_DK = 128
GDN_DV = 128
CONV_K = 4
DSA_HEADS = 8
DSA_DH = 128
KV_RANK = 256
IDX_HEADS = 8
IDX_DIM = 64
TOPK_MAX = 256
NUM_BUCKETS = 32
MAX_DISTANCE = 128
EPS = 1e-6

LANES = 128
SUBLANES = 8
DSA_SUPER = 4
ONES_ROWS = 16
GDN_W = GDN_HEADS * GDN_DV
DSA_W = DSA_HEADS * DSA_DH
LOG2E = math.log2(math.e)

F32 = jnp.float32
BF16 = jnp.bfloat16
INT_MIN = -2 ** 31
VMEM_LIMIT = 48 * 1024 * 1024

_NT = (((1,), (1,)), ((), ()))


def _dot(a, b):
    return jnp.dot(a, b, preferred_element_type=F32)


def _dot_nt(a, b):
    return lax.dot_general(a, b, _NT, preferred_element_type=F32)


def _split2(a):
    hi = a.astype(BF16)
    lo = (a - hi.astype(F32)).astype(BF16)
    return hi, lo


def _mm3(a, b):
    ah, al = _split2(a)
    bh, bl = _split2(b)
    n = b.shape[1]
    big = _dot(ah, jnp.concatenate([bh, bl], axis=1))
    return big[:, :n] + (big[:, n:] + _dot(al, bh))


def _sigmoid(x):
    return 0.5 * jnp.tanh(0.5 * x) + 0.5


def _silu(x):
    h = 0.5 * x
    return h * jnp.tanh(h) + h


def _softplus(x):
    return jnp.maximum(x, 0.0) + jnp.log1p(jnp.exp(-jnp.abs(x)))


def _fold_rows(x, op):
    parts = [x[r:r + SUBLANES, :] for r in range(0, x.shape[0], SUBLANES)]
    while len(parts) > 1:
        nxt = [op(parts[j], parts[j + 1]) for j in range(0, len(parts) - 1, 2)]
        if len(parts) % 2:
            nxt.append(parts[-1])
        parts = nxt
    return parts[0]


def _mod_kernel(c_ref, w_ref, b_ref, o_ref):
    a = _silu(c_ref[...])
    o_ref[...] = _mm3(a, w_ref[...]) + b_ref[...]


def _adaln_mod(c, w_ada, b_ada):
    bsz, d = c.shape
    n = w_ada.shape[1]
    tn = 512
    return pl.pallas_call(
        _mod_kernel,
        grid=(n // tn,),
        in_specs=[pl.BlockSpec((bsz, d), lambda j: (0, 0)),
                  pl.BlockSpec((d, tn), lambda j: (0, j)),
                  pl.BlockSpec((1, tn), lambda j: (0, j))],
        out_specs=pl.BlockSpec((bsz, tn), lambda j: (0, j)),
        out_shape=jax.ShapeDtypeStruct((bsz, n), F32),
        name="adaln_mod",
    )(c, w_ada, b_ada.reshape(1, n))


def _h_kernel(x_ref, g_ref, sc_ref, sh_ref, o_ref):
    x = x_ref[...]
    y = x * lax.rsqrt(jnp.mean(x * x, axis=-1, keepdims=True) + EPS)
    y = y * g_ref[...]
    o_ref[...] = (y * (1.0 + sc_ref[0]) + sh_ref[0]).astype(o_ref.dtype)


def _modulated_norm(x2, norm_g, scale, shift, seq):
    m, d = x2.shape
    bsz = m // seq
    tm = 512
    nt = seq // tm
    return pl.pallas_call(
        _h_kernel,
        grid=(bsz, nt),
        in_specs=[pl.BlockSpec((tm, d), lambda b, i: (b * nt + i, 0)),
                  pl.BlockSpec((1, d), lambda b, i: (0, 0)),
                  pl.BlockSpec((1, 1, d), lambda b, i: (b, 0, 0)),
                  pl.BlockSpec((1, 1, d), lambda b, i: (b, 0, 0))],
        out_specs=pl.BlockSpec((tm, d), lambda b, i: (b * nt + i, 0)),
        out_shape=jax.ShapeDtypeStruct((m, d), BF16),
        name="modulated_norm",
    )(x2, norm_g.reshape(1, d), scale.reshape(bsz, 1, d), shift.reshape(bsz, 1, d))


def _proj_kernel(h_ref, w_ref, o_ref):
    o_ref[...] = _dot(h_ref[...], w_ref[...]).astype(o_ref.dtype)


def _project(h, w, out_dtype, tn, tm=1024):
    m, k = h.shape
    n = w.shape[1]
    return pl.pallas_call(
        _proj_kernel,
        grid=(m // tm, n // tn),
        in_specs=[pl.BlockSpec((tm, k), lambda i, j: (i, 0)),
                  pl.BlockSpec((k, tn), lambda i, j: (0, j))],
        out_specs=pl.BlockSpec((tm, tn), lambda i, j: (i, j)),
        out_shape=jax.ShapeDtypeStruct((m, n), out_dtype),
        compiler_params=pltpu.CompilerParams(vmem_limit_bytes=VMEM_LIMIT),
        name="in_proj",
    )(h, w)


def _unit_lower_inverses(a_list, between=None):
    n = a_list[0].shape[0]
    ri = lax.broadcasted_iota(jnp.int32, (n, n), 0)
    ci = lax.broadcasted_iota(jnp.int32, (n, n), 1)
    x = ri ^ ci
    eye = (ri == ci).astype(F32)
    ts = [eye - jnp.where(x == 1, a, 0.0) for a in a_list]
    s = 2
    while s < n:
        level = (x >= s) & (x < 2 * s)
        tb = [t.astype(BF16) for t in ts]
        ps = [_dot(jnp.where(level, a, 0.0).astype(BF16), t) for a, t in zip(a_list, tb)]
        if between is not None:
            between()
        ts = [t - _dot(t16, p.astype(BF16)) for t, t16, p in zip(ts, tb, ps)]
        if between is not None:
            between()
        s *= 2
    return ts


def _gdn_kernel(q_ref, k_ref, v_ref, misc_ref, cwq_ref, cwk_ref, cwv_ref,
                alog_ref, dtb_ref, z_ref, ng_ref, o_ref, qs, ks, vs, ds, bs,
                qws, nws, cus, ous, edl):
    s = pl.program_id(0)
    n_items = pl.num_programs(0) - 2
    seq = q_ref.shape[0]
    c = LANES
    nch = seq // c
    wr = s % 2
    rd = (s + 1) % 2
    hd = jnp.minimum(s, n_items - 1) % GDN_HEADS
    row8 = lax.broadcasted_iota(jnp.int32, (SUBLANES, c), 0)
    ri = lax.broadcasted_iota(jnp.int32, (c, c), 0)
    ci = lax.broadcasted_iota(jnp.int32, (c, c), 1)
    tri = (ri >= ci).astype(BF16)
    ng = ng_ref[...]
    rows = [slice(n * c, (n + 1) * c) for n in range(nch)]

    @pl.when(s == 0)
    def _():
        for ref in (qs, ks, vs, ds, bs, cus, ous):
            ref[rd] = jnp.zeros((seq, c), F32)
        for ref in (qws, nws):
            ref[rd] = jnp.zeros((seq, c), BF16)
        edl[rd] = jnp.zeros((nch, SUBLANES, c), F32)

    def prologue(n):
        def conv_silu(u_ref, w_ref):
            u = u_ref[rows[n], :]
            w = w_ref[...]
            y = u * w[CONV_K - 1:CONV_K, :]
            for k in range(1, CONV_K):
                if n > 0:
                    sh = u_ref[n * c - k:(n + 1) * c - k, :]
                else:
                    sh = pltpu.roll(u, k, 0)
                    head = jnp.where(row8 < k, 0.0, sh[:SUBLANES, :])
                    sh = jnp.concatenate([head, sh[SUBLANES:, :]], axis=0)
                y = y + sh * w[CONV_K - 1 - k:CONV_K - k, :]
            return _silu(y)

        def l2n(x):
            return x * lax.rsqrt(jnp.sum(x * x, axis=-1, keepdims=True) + EPS)

        qs[wr, rows[n], :] = l2n(conv_silu(q_ref, cwq_ref)) * (GDN_DK ** -0.5)
        ks[wr, rows[n], :] = l2n(conv_silu(k_ref, cwk_ref))
        vs[wr, rows[n], :] = conv_silu(v_ref, cwv_ref)
        misc = misc_ref[rows[n], :]
        aa = jnp.sum(jnp.where(ci == hd, misc, 0.0), axis=-1, keepdims=True)
        ba = jnp.sum(jnp.where(ci == hd + GDN_HEADS, misc, 0.0), axis=-1, keepdims=True)
        bs[wr, rows[n], :] = jnp.broadcast_to(_sigmoid(ba), (c, c))
        g = -jnp.exp(alog_ref[0]) * _softplus(aa + dtb_ref[0])
        g1 = g.astype(BF16)
        r1 = g - g1.astype(F32)
        g2 = r1.astype(BF16)
        g3 = (r1 - g2.astype(F32)).astype(BF16)
        d = _dot(tri, jnp.concatenate([g1, g2, g3], axis=1))
        ds[wr, rows[n], :] = d[:, :c] + (d[:, c:2 * c] + d[:, 2 * c:])

    def recur(n, st):
        sb = st.astype(BF16)
        o = _dot(qws[rd, rows[n], :], sb) + ous[rd, rows[n], :]
        st = st * edl[rd, n][0:1, :] + (cus[rd, rows[n], :] - _dot(nws[rd, rows[n], :], sb))
        on = o * lax.rsqrt(jnp.mean(o * o, axis=-1, keepdims=True) + EPS) * ng
        o_ref[rows[n], :] = (on * _silu(z_ref[rows[n], :])).astype(o_ref.dtype)
        return st

    carry = [jnp.zeros((GDN_DK, GDN_DV), F32), 0]

    def hosted():
        n = carry[1]
        carry[0] = recur(n, carry[0])
        prologue(n)
        carry[1] = n + 1

    dcols = [ds[rd, r, :] for r in rows]
    ks_ = [ks[rd, r, :] for r in rows]
    kbs = [k * bs[rd, r, :] for k, r in zip(ks_, rows)]
    kq = [_dot_nt(jnp.concatenate([kb, qs[rd, r, :]], axis=0).astype(BF16), k.astype(BF16))
          for kb, k, r in zip(kbs, ks_, rows)]
    hosted()
    gammas = [jnp.exp(jnp.where(ri >= ci, d - d.T, -jnp.inf)) for d in dcols]
    a_list = [jnp.where(ri > ci, m[:c] * gm, 0.0) for m, gm in zip(kq, gammas)]
    attns = [(m[c:] * gm).astype(BF16) for m, gm in zip(kq, gammas)]
    ecols = [jnp.exp(d) for d in dcols]
    rhs = [jnp.concatenate([kb * e, vs[rd, r, :] * bs[rd, r, :]], axis=1)
           for r, kb, e in zip(rows, kbs, ecols)]
    hosted()
    tinvs = [t.astype(BF16) for t in _unit_lower_inverses(a_list, between=hosted)]
    x0 = [_dot(t, r.astype(BF16)) for t, r in zip(tinvs, rhs)]
    hosted()
    res = [r - (x + _mm3(a, x)) for r, x, a in zip(rhs, x0, a_list)]
    hosted()
    assert carry[1] == nch
    wu = [(x + _dot(t, e.astype(BF16))).astype(BF16) for x, t, e in zip(x0, tinvs, res)]
    kdts = [(k * jnp.exp(d[c - 1:c, :] - d)).T.astype(BF16) for k, d in zip(ks_, dcols)]
    aw = [_dot(at, x) for at, x in zip(attns, wu)]
    kw = [_dot(kd, x) for kd, x in zip(kdts, wu)]
    for n, (r, d, e, a_, k_) in enumerate(zip(rows, dcols, ecols, aw, kw)):
        qws[wr, r, :] = (qs[rd, r, :] * e - a_[:, :c]).astype(BF16)
        ous[wr, r, :] = a_[:, c:]
        nws[wr, r, :] = k_[:, :c].astype(BF16)
        cus[wr, r, :] = k_[:, c:]
        edl[wr, n] = jnp.broadcast_to(jnp.exp(d[c - 1:c, :]), (SUBLANES, c))


def _gdn(pa, pb, conv_w, a_log, dt_bias, gdn_norm_g, bsz, seq):
    nh = GDN_HEADS
    c = LANES
    nch = seq // c
    assert nch == 2 + 2 * (int(math.log2(c)) - 1) + 2
    n_items = bsz * nh
    alog_b = jnp.broadcast_to(a_log.reshape(nh, 1, 1), (nh, 1, c))
    dtb_b = jnp.broadcast_to(dt_bias.reshape(nh, 1, 1), (nh, 1, c))
    pro = lambda s: jnp.minimum(s, n_items - 1)
    rec = lambda s: jnp.clip(s - 2, 0, n_items - 1)
    blk = lambda off: pl.BlockSpec((seq, c), lambda s: (pro(s) // nh, off + pro(s) % nh))
    cw = lambda off: pl.BlockSpec((CONV_K, c), lambda s: (0, off + pro(s) % nh))
    hv = pl.BlockSpec((1, 1, c), lambda s: (pro(s) % nh, 0, 0))
    return pl.pallas_call(
        _gdn_kernel,
        grid=(n_items + 2,),
        in_specs=[blk(0), blk(nh), blk(2 * nh),
                  pl.BlockSpec((seq, c), lambda s: (pro(s) // nh, 2)),
                  cw(0), cw(nh), cw(2 * nh), hv, hv,
                  pl.BlockSpec((seq, c), lambda s: (rec(s) // nh, 3 * nh + rec(s) % nh)),
                  pl.BlockSpec((1, c), lambda s: (0, 0))],
        out_specs=pl.BlockSpec((seq, c), lambda s: (rec(s) // nh, rec(s) % nh)),
        out_shape=jax.ShapeDtypeStruct((bsz * seq, GDN_W), BF16),
        scratch_shapes=([pltpu.VMEM((2, seq, c), F32)] * 5
                        + [pltpu.VMEM((2, seq, c), BF16)] * 2
                        + [pltpu.VMEM((2, seq, c), F32)] * 2
                        + [pltpu.VMEM((2, nch, SUBLANES, c), F32)]),
        compiler_params=pltpu.CompilerParams(vmem_limit_bytes=VMEM_LIMIT),
        name="gated_deltanet",
    )(pa, pa, pa, pb, conv_w, conv_w, conv_w, alog_b, dtb_b, pa,
      gdn_norm_g.reshape(1, c))


def _t5_bucket_table(n_tiles):
    s = np.arange(LANES)[:, None]
    q = np.arange(LANES)[None, :]
    out = []
    max_exact = NUM_BUCKETS // 2
    for d in range(n_tiles):
        n = np.maximum(d * LANES + q - s, 0)
        nf = np.maximum(n, 1).astype(np.float32)
        large = max_exact + (np.log(nf / max_exact) / math.log(MAX_DISTANCE / max_exact)
                             * (NUM_BUCKETS - max_exact)).astype(np.int32)
        large = np.minimum(large, NUM_BUCKETS - 1)
        out.append(np.where(n < max_exact, n, large))
    return np.stack(out).astype(np.int32)


_BIAS_TILES = 3
assert (_BIAS_TILES - 1) * LANES - (LANES - 1) >= MAX_DISTANCE


def _bias_kernel(bkt_ref, rel_ref, o_ref):
    hd = pl.program_id(0)
    for d in range(_BIAS_TILES):
        bkt = bkt_ref[d]
        acc = jnp.zeros((LANES, LANES), F32)
        for j in range(NUM_BUCKETS):
            acc = jnp.where(bkt == j, rel_ref[j, hd] * LOG2E, acc)
        o_ref[d, 0] = acc


def _bias_tiles(rel_bias):
    bkt = jnp.asarray(_t5_bucket_table(_BIAS_TILES))
    return pl.pallas_call(
        _bias_kernel,
        grid=(DSA_HEADS,),
        in_specs=[pl.BlockSpec((_BIAS_TILES, LANES, LANES), lambda h: (0, 0, 0)),
                  pl.BlockSpec(memory_space=pltpu.SMEM)],
        out_specs=pl.BlockSpec((_BIAS_TILES, 1, LANES, LANES), lambda h: (0, h, 0, 0)),
        out_shape=jax.ShapeDtypeStruct((_BIAS_TILES, DSA_HEADS, LANES, LANES), F32),
        name="t5_bias_tiles",
    )(bkt, rel_bias)


def _dsa_kernel(qb_ref, qi_ref, misc_ref, zb_ref, kil_ref, kih_ref, ckv_ref,
                kvg_ref, wuk_ref, wuvt_ref, bias_ref, o_ref,
                ckvn, ckvt, keys, madd, qlat, lg, pbuf, acc):
    i = pl.program_id(1)
    c = LANES
    sc_w = DSA_SUPER * c
    nsc = i // DSA_SUPER + 1
    nh = DSA_HEADS
    seq = ckv_ref.shape[0]

    @pl.when(i == 0)
    def _():
        x = ckv_ref[...]
        y = x * lax.rsqrt(jnp.mean(x * x, axis=-1, keepdims=True) + EPS) * kvg_ref[...]
        ckvn[...] = y.astype(BF16)
        for r in range(seq // c):
            ckvt[r // DSA_SUPER, :KV_RANK, (r % DSA_SUPER) * c:(r % DSA_SUPER + 1) * c] = (
                y[r * c:(r + 1) * c, :].T.astype(BF16))
        ckvt[:, KV_RANK:, :] = jnp.ones((seq // sc_w, ONES_ROWS, sc_w), BF16)

    ri = lax.broadcasted_iota(jnp.int32, (c, c), 0)
    ci = lax.broadcasted_iota(jnp.int32, (c, c), 1)
    misc_t = misc_ref[...].T
    wscale = IDX_HEADS ** -0.5 * IDX_DIM ** -0.5
    wrow = [misc_t[2 * GDN_HEADS + h:2 * GDN_HEADS + h + 1, :] * wscale
            for h in range(IDX_HEADS)]
    npair = IDX_HEADS // 2
    qi4 = jnp.concatenate([qi_ref[:, p * c:(p + 1) * c] for p in range(npair)], axis=0)

    def score_super(s, carry):
        ccs = [s * DSA_SUPER + j for j in range(DSA_SUPER)]
        sls = [pl.ds(pl.multiple_of(cc * c, c), c) for cc in ccs]
        dots = [_dot_nt(jnp.concatenate([kil_ref[sl, :], kih_ref[sl, :]], axis=0), qi4)
                for sl in sls]
        for cc, d in zip(ccs, dots):
            acc_s = jnp.zeros((c, c), F32)
            for p in range(npair):
                acc_s = acc_s + wrow[2 * p] * jnp.maximum(d[:c, p * c:(p + 1) * c], 0.0)
                acc_s = acc_s + wrow[2 * p + 1] * jnp.maximum(d[c:, p * c:(p + 1) * c], 0.0)
            adm = (cc * c + ri) <= (i * c + ci)
            acc_s = jnp.where(adm, acc_s, -jnp.inf)
            bits = lax.bitcast_convert_type(acc_s, jnp.int32)
            keys[cc] = jnp.where(bits < 0, bits ^ 0x7FFFFFFF, bits)
        return carry

    lax.fori_loop(0, nsc, score_super, 0)

    kf = float(TOPK_MAX)

    def count(pred):
        def body(s, cnt):
            parts = [_fold_rows(jnp.where(pred(keys[s * DSA_SUPER + j]), 1.0, 0.0), jnp.add)
                     for j in range(DSA_SUPER)]
            return cnt + ((parts[0] + parts[1]) + (parts[2] + parts[3]))
        cnt = lax.fori_loop(0, nsc, body, jnp.zeros((SUBLANES, c), F32))
        return jnp.sum(cnt, axis=0, keepdims=True)

    def bit_step(it, carry):
        v, n_v = carry
        cand = v + jnp.left_shift(jnp.int32(1), 31 - it)
        tot = count(lambda kc: kc >= cand)
        take = tot >= kf
        return jnp.where(take, cand, v), jnp.where(take, tot, n_v)

    n_all = jnp.full((1, c), 1.0, F32) * (nsc * sc_w).astype(F32)
    thr, n_ge = lax.fori_loop(0, 32, bit_step,
                              (jnp.full((1, c), INT_MIN, jnp.int32), n_all))
    n_gt = count(lambda kc: kc > thr)
    need = kf - n_gt
    any_ties = jnp.max(n_ge - n_gt) > 1.0
    lower = (ci < ri).astype(BF16)

    def mask_super(s, running):
        ccs = [s * DSA_SUPER + j for j in range(DSA_SUPER)]
        kcs = [keys[cc] for cc in ccs]
        eqs = [jnp.where(kc == thr, 1.0, 0.0) for kc in kcs]
        prefix = [_dot(lower, eq.astype(BF16)) for eq in eqs]
        totals = [jnp.sum(_fold_rows(eq, jnp.add), axis=0, keepdims=True) for eq in eqs]
        for cc, kc, eq, pre, tot in zip(ccs, kcs, eqs, prefix, totals):
            sel = jnp.where(kc > thr, 1.0, jnp.where(running + pre < need, eq, 0.0))
            adm = (cc * c + ri) <= (i * c + ci)
            madd[cc] = jnp.where(adm & (sel > 0.5), 0.0, -jnp.inf)
            running = running + tot
        return running

    @pl.when(any_ties)
    def _():
        lax.fori_loop(0, nsc, mask_super, jnp.zeros((1, c), F32))

    @pl.when(jnp.logical_not(any_ties))
    def _():
        def mask_chunk(cc, carry):
            adm = (cc * c + ri) <= (i * c + ci)
            madd[cc] = jnp.where(adm & (keys[cc] >= thr), 0.0, -jnp.inf)
            return carry
        lax.fori_loop(0, nsc * DSA_SUPER, mask_chunk, 0)

    for h in range(nh):
        qh = qb_ref[:, h * DSA_DH:(h + 1) * DSA_DH]
        qlat[h * c:(h + 1) * c, :] = (
            _dot_nt(qh, wuk_ref[h]) * (DSA_DH ** -0.5 * LOG2E)).astype(BF16)

    acc[...] = jnp.zeros(acc.shape, F32)

    def logit_super(s, mx):
        kv = ckvn[pl.ds(pl.multiple_of(s * sc_w, sc_w), sc_w), :]
        l_all = _dot_nt(kv, qlat[...])
        new = []
        for h in range(nh):
            mh = mx[:, h * c:(h + 1) * c]
            for j in range(DSA_SUPER):
                cc = s * DSA_SUPER + j
                bidx = jnp.clip(i - cc, 0, _BIAS_TILES - 1)
                l = (l_all[j * c:(j + 1) * c, h * c:(h + 1) * c]
                     + bias_ref[bidx, h] + madd[cc])
                lg[s, j * c:(j + 1) * c, h * c:(h + 1) * c] = l
                mh = jnp.maximum(mh, _fold_rows(l, jnp.maximum))
            new.append(mh)
        return jnp.concatenate(new, axis=1)

    mx = lax.fori_loop(0, nsc, logit_super,
                       jnp.full((SUBLANES, nh * c), -jnp.inf, F32))
    m = jnp.max(mx, axis=0, keepdims=True)

    def pv_super(s, carry):
        for h in range(nh):
            mh = m[:, h * c:(h + 1) * c]
            for j in range(DSA_SUPER):
                p = jnp.exp2(lg[s, j * c:(j + 1) * c, h * c:(h + 1) * c] - mh)
                pbuf[j * c:(j + 1) * c, h * c:(h + 1) * c] = p.astype(BF16)
        acc[...] = acc[...] + _dot(ckvt[s], pbuf[...])
        return carry

    lax.fori_loop(0, nsc, pv_super, 0)
    denom = acc[KV_RANK:KV_RANK + 1, :]

    for h in range(nh):
        cols = slice(h * c, (h + 1) * c)
        olat_t = (acc[:KV_RANK, cols] / denom[:, cols]).astype(BF16)
        ob = _dot(wuvt_ref[h], olat_t).T
        zb = zb_ref[:, h * DSA_DH:(h + 1) * DSA_DH]
        o_ref[:, h * DSA_DH:(h + 1) * DSA_DH] = (ob * _silu(zb)).astype(o_ref.dtype)


def _dsa(pa, pb, pc, kv_norm_g, wuk_t, wuv_t, bias_tiles, bsz, seq):
    c = LANES
    nb = seq // c
    nsc = nb // DSA_SUPER
    rows_all = DSA_HEADS * c
    za_off = (4 * GDN_W + 2 * D_MODEL) // DSA_W
    qi_off = DSA_W // (IDX_HEADS * IDX_DIM)
    kil_off = (DSA_W + IDX_HEADS * IDX_DIM) // c
    row = lambda b, i: b * nb + i
    return pl.pallas_call(
        _dsa_kernel,
        grid=(bsz, nb),
        in_specs=[
            pl.BlockSpec((c, DSA_W), lambda b, i: (row(b, i), 0)),
            pl.BlockSpec((c, IDX_HEADS * IDX_DIM), lambda b, i: (row(b, i), qi_off)),
            pl.BlockSpec((c, c), lambda b, i: (row(b, i), 2)),
            pl.BlockSpec((c, DSA_W), lambda b, i: (row(b, i), za_off)),
            pl.BlockSpec((seq, c), lambda b, i: (b, kil_off)),
            pl.BlockSpec((seq, c), lambda b, i: (b, kil_off + 1)),
            pl.BlockSpec((seq, KV_RANK), lambda b, i: (b, 0)),
            pl.BlockSpec((1, KV_RANK), lambda b, i: (0, 0)),
            pl.BlockSpec((DSA_HEADS, KV_RANK, DSA_DH), lambda b, i: (0, 0, 0)),
            pl.BlockSpec((DSA_HEADS, DSA_DH, KV_RANK), lambda b, i: (0, 0, 0)),
            pl.BlockSpec((_BIAS_TILES, DSA_HEADS, c, c), lambda b, i: (0, 0, 0, 0)),
        ],
        out_specs=pl.BlockSpec((c, DSA_W), lambda b, i: (row(b, i), 0)),
        out_shape=jax.ShapeDtypeStruct((bsz * seq, DSA_W), BF16),
        scratch_shapes=[pltpu.VMEM((seq, KV_RANK), BF16),
                        pltpu.VMEM((nsc, KV_RANK + ONES_ROWS, DSA_SUPER * c), BF16),
                        pltpu.VMEM((nb, c, c), jnp.int32),
                        pltpu.VMEM((nb, c, c), F32),
                        pltpu.VMEM((rows_all, KV_RANK), BF16),
                        pltpu.VMEM((nsc, DSA_SUPER * c, rows_all), F32),
                        pltpu.VMEM((DSA_SUPER * c, rows_all), BF16),
                        pltpu.VMEM((KV_RANK + ONES_ROWS, rows_all), F32)],
        compiler_params=pltpu.CompilerParams(vmem_limit_bytes=VMEM_LIMIT),
        name="dsa_attention",
    )(pc, pc, pb, pa, pc, pc, pb, kv_norm_g.reshape(1, KV_RANK), wuk_t, wuv_t,
      bias_tiles)


def _out_kernel(oa_ref, ob_ref, gl_ref, x_ref, gate_ref, woa_ref, wob_ref, wout_ref,
                fg_ref, o_ref):
    d = x_ref.shape[1]
    ya = _dot(oa_ref[...], woa_ref[...])
    yb = _dot(ob_ref[...], wob_ref[...])
    gl = gl_ref[...]
    mix = _sigmoid(gl[:, :d]) * ya + _sigmoid(gl[:, d:]) * yb
    out = _dot(mix.astype(BF16), wout_ref[...])
    xn = x_ref[...] + gate_ref[0] * out
    y = xn * lax.rsqrt(jnp.mean(xn * xn, axis=-1, keepdims=True) + EPS)
    o_ref[...] = y * fg_ref[...]


def _merge_out(oa, ob, pa, x2, gate, w_o_gdn, w_o_dsa, w_out, final_g, bsz, seq):
    d = D_MODEL
    tm = 512
    nt = seq // tm
    gl_off = (4 * GDN_W) // (2 * d)
    row = lambda b, i: (b * nt + i, 0)
    full = lambda b, i: (0, 0)
    return pl.pallas_call(
        _out_kernel,
        grid=(bsz, nt),
        in_specs=[pl.BlockSpec((tm, GDN_W), row),
                  pl.BlockSpec((tm, DSA_W), row),
                  pl.BlockSpec((tm, 2 * d), lambda b, i: (b * nt + i, gl_off)),
                  pl.BlockSpec((tm, d), row),
                  pl.BlockSpec((1, 1, d), lambda b, i: (b, 0, 0)),
                  pl.BlockSpec((GDN_W, d), full),
                  pl.BlockSpec((DSA_W, d), full),
                  pl.BlockSpec((d, d), full),
                  pl.BlockSpec((1, d), full)],
        out_specs=pl.BlockSpec((tm, d), row),
        out_shape=jax.ShapeDtypeStruct((bsz * seq, d), F32),
        compiler_params=pltpu.CompilerParams(vmem_limit_bytes=VMEM_LIMIT),
        name="merge_out",
    )(oa, ob, pa, x2, gate.reshape(bsz, 1, d), w_o_gdn, w_o_dsa, w_out,
      final_g.reshape(1, d))


def _split_w_in(w):
    o = 0
    names = ("qa", "ka", "va", "za", "aa", "ba", "qb", "ckv", "zb", "qi", "ki", "wi", "gl")
    sizes = (GDN_HEADS * GDN_DK, GDN_HEADS * GDN_DK, GDN_W, GDN_W, GDN_HEADS, GDN_HEADS,
             DSA_W, KV_RANK, DSA_W, IDX_HEADS * IDX_DIM, IDX_DIM, IDX_HEADS, 2 * D_MODEL)
    w = w.astype(BF16)
    col = {}
    for nm, sz in zip(names, sizes):
        col[nm] = w[:, o:o + sz]
        o += sz
    k = w.shape[0]
    z = lambda n: jnp.zeros((k, n), w.dtype)
    wa = jnp.concatenate([col["qa"], col["ka"], col["va"], col["za"], col["gl"], col["zb"]], 1)
    n_misc = 2 * GDN_HEADS + IDX_HEADS
    wb = jnp.concatenate([col["ckv"], col["aa"], col["ba"], col["wi"], z(LANES - n_misc)], 1)
    pad = z(LANES - IDX_DIM)
    wc = jnp.concatenate([col["qb"], col["qi"], col["ki"], pad, pad, col["ki"]], 1)
    return wa, wb, wc


def kernel(x, c, norm_g, w_ada, b_ada, w_in, conv_w, a_log, dt_bias, gdn_norm_g, w_o_gdn,
           kv_norm_g, w_uk, w_uv, w_o_dsa, w_out, rel_bias, final_g):
    bsz, seq, d = x.shape
    depth = norm_g.shape[0]
    assert depth == 1, "the final RMSNorm is fused into the layer's merge kernel"
    x2 = x.reshape(bsz * seq, d)
    bias_tiles = _bias_tiles(rel_bias)
    for l in range(depth):
        mod = _adaln_mod(c, w_ada[l], b_ada[l])
        shift, scale, gate = mod[:, :d], mod[:, d:2 * d], mod[:, 2 * d:]
        h = _modulated_norm(x2, norm_g[l], scale, shift, seq)
        wa, wb, wc = _split_w_in(w_in[l])
        pa = _project(h, wa, F32, 1024, tm=2048)
        pb = _project(h, wb, F32, wb.shape[1], tm=2048)
        pc = _project(h, wc, BF16, wc.shape[1], tm=2048)
        oa = _gdn(pa, pb, conv_w[l], a_log[l], dt_bias[l], gdn_norm_g[l], bsz, seq)
        wuk_t = jnp.transpose(w_uk[l], (1, 0, 2)).astype(BF16)
        wuv_t = jnp.transpose(w_uv[l], (1, 2, 0)).astype(BF16)
        ob = _dsa(pa, pb, pc, kv_norm_g[l], wuk_t, wuv_t, bias_tiles, bsz, seq)
        x2 = _merge_out(oa, ob, pa, x2, gate, w_o_gdn[l].astype(BF16),
                        w_o_dsa[l].astype(BF16), w_out[l].astype(BF16), final_g,
                        bsz, seq)
    return x2.reshape(bsz, seq, d)
```

```python
import math

import numpy as np
import jax
import jax.numpy as jnp
from jax import lax
from jax.experimental import pallas as pl
from jax.experimental.pallas import tpu as pltpu

D_MODEL = 1024
GDN_HEADS = 8
GDN_DK = 128
GDN_DV = 128
CONV_K = 4
DSA_HEADS = 8
DSA_DH = 128
KV_RANK = 256
IDX_HEADS = 8
IDX_DIM = 64
TOPK_MAX = 256
NUM_BUCKETS = 32
MAX_DISTANCE = 128
EPS = 1e-6

LANES = 128
SUBLANES = 8
DSA_SUPER = 4
ONES_ROWS = 16
BISECT_CHECK = 25
GDN_W = GDN_HEADS * GDN_DV
DSA_W = DSA_HEADS * DSA_DH
LOG2E = math.log2(math.e)

F32 = jnp.float32
BF16 = jnp.bfloat16
INT_MIN = -2 ** 31
VMEM_LIMIT = 48 * 1024 * 1024

_NT = (((1,), (1,)), ((), ()))


def _dot(a, b):
    return jnp.dot(a, b, preferred_element_type=F32)


def _dot_nt(a, b):
    return lax.dot_general(a, b, _NT, preferred_element_type=F32)


def _split2(a):
    hi = a.astype(BF16)
    lo = (a - hi.astype(F32)).astype(BF16)
    return hi, lo


def _mm3(a, b):
    ah, al = _split2(a)
    bh, bl = _split2(b)
    n = b.shape[1]
    big = _dot(ah, jnp.concatenate([bh, bl], axis=1))
    return big[:, :n] + (big[:, n:] + _dot(al, bh))


def _sigmoid(x):
    return 0.5 * jnp.tanh(0.5 * x) + 0.5


def _silu(x):
    h = 0.5 * x
    return h * jnp.tanh(h) + h


def _softplus(x):
    return jnp.maximum(x, 0.0) + jnp.log1p(jnp.exp(-jnp.abs(x)))


def _fold_rows(x, op):
    parts = [x[r:r + SUBLANES, :] for r in range(0, x.shape[0], SUBLANES)]
    while len(parts) > 1:
        nxt = [op(parts[j], parts[j + 1]) for j in range(0, len(parts) - 1, 2)]
        if len(parts) % 2:
            nxt.append(parts[-1])
        parts = nxt
    return parts[0]


def _mod_kernel(c_ref, w_ref, b_ref, o_ref):
    a = _silu(c_ref[...])
    o_ref[...] = _mm3(a, w_ref[...]) + b_ref[...]


def _adaln_mod(c, w_ada, b_ada):
    bsz, d = c.shape
    n = w_ada.shape[1]
    tn = 512
    return pl.pallas_call(
        _mod_kernel,
        grid=(n // tn,),
        in_specs=[pl.BlockSpec((bsz, d), lambda j: (0, 0)),
                  pl.BlockSpec((d, tn), lambda j: (0, j)),
                  pl.BlockSpec((1, tn), lambda j: (0, j))],
        out_specs=pl.BlockSpec((bsz, tn), lambda j: (0, j)),
        out_shape=jax.ShapeDtypeStruct((bsz, n), F32),
        name="adaln_mod",
    )(c, w_ada, b_ada.reshape(1, n))


def _h_kernel(x_ref, g_ref, sc_ref, sh_ref, o_ref):
    x = x_ref[...]
    y = x * lax.rsqrt(jnp.mean(x * x, axis=-1, keepdims=True) + EPS)
    y = y * g_ref[...]
    o_ref[...] = (y * (1.0 + sc_ref[0]) + sh_ref[0]).astype(o_ref.dtype)


def _modulated_norm(x2, norm_g, scale, shift, seq):
    m, d = x2.shape
    bsz = m // seq
    tm = 512
    nt = seq // tm
    return pl.pallas_call(
        _h_kernel,
        grid=(bsz, nt),
        in_specs=[pl.BlockSpec((tm, d), lambda b, i: (b * nt + i, 0)),
                  pl.BlockSpec((1, d), lambda b, i: (0, 0)),
                  pl.BlockSpec((1, 1, d), lambda b, i: (b, 0, 0)),
                  pl.BlockSpec((1, 1, d), lambda b, i: (b, 0, 0))],
        out_specs=pl.BlockSpec((tm, d), lambda b, i: (b * nt + i, 0)),
        out_shape=jax.ShapeDtypeStruct((m, d), BF16),
        name="modulated_norm",
    )(x2, norm_g.reshape(1, d), scale.reshape(bsz, 1, d), shift.reshape(bsz, 1, d))


def _proj_kernel(h_ref, w_ref, o_ref):
    o_ref[...] = _dot(h_ref[...], w_ref[...]).astype(o_ref.dtype)


def _project(h, w, out_dtype, tn, tm=1024):
    m, k = h.shape
    n = w.shape[1]
    return pl.pallas_call(
        _proj_kernel,
        grid=(m // tm, n // tn),
        in_specs=[pl.BlockSpec((tm, k), lambda i, j: (i, 0)),
                  pl.BlockSpec((k, tn), lambda i, j: (0, j))],
        out_specs=pl.BlockSpec((tm, tn), lambda i, j: (i, j)),
        out_shape=jax.ShapeDtypeStruct((m, n), out_dtype),
        compiler_params=pltpu.CompilerParams(vmem_limit_bytes=VMEM_LIMIT),
        name="in_proj",
    )(h, w)


def _unit_lower_inverses(a_list, between=None):
    n = a_list[0].shape[0]
    ri = lax.broadcasted_iota(jnp.int32, (n, n), 0)
    ci = lax.broadcasted_iota(jnp.int32, (n, n), 1)
    x = ri ^ ci
    eye = (ri == ci).astype(F32)
    ts = [eye - jnp.where(x == 1, a, 0.0) for a in a_list]
    s = 2
    while s < n:
        level = (x >= s) & (x < 2 * s)
        tb = [t.astype(BF16) for t in ts]
        ps = [_dot(jnp.where(level, a, 0.0).astype(BF16), t) for a, t in zip(a_list, tb)]
        if between is not None:
            between()
        ts = [t - _dot(t16, p.astype(BF16)) for t, t16, p in zip(ts, tb, ps)]
        if between is not None:
            between()
        s *= 2
    return ts


def _gdn_kernel(q_ref, k_ref, v_ref, misc_ref, cwq_ref, cwk_ref, cwv_ref,
                alog_ref, dtb_ref, z_ref, ng_ref, o_ref, qs, ks, vs, ds, bs,
                qws, nws, cus, ous, edl):
    s = pl.program_id(0)
    n_items = pl.num_programs(0) - 2
    seq = q_ref.shape[0]
    c = LANES
    nch = seq // c
    wr = s % 2
    rd = (s + 1) % 2
    hd = jnp.minimum(s, n_items - 1) % GDN_HEADS
    row8 = lax.broadcasted_iota(jnp.int32, (SUBLANES, c), 0)
    ri = lax.broadcasted_iota(jnp.int32, (c, c), 0)
    ci = lax.broadcasted_iota(jnp.int32, (c, c), 1)
    tri = (ri >= ci).astype(BF16)
    ng = ng_ref[...]
    rows = [slice(n * c, (n + 1) * c) for n in range(nch)]

    @pl.when(s == 0)
    def _():
        for ref in (qs, ks, vs, ds, bs, cus, ous):
            ref[rd] = jnp.zeros((seq, c), F32)
        for ref in (qws, nws):
            ref[rd] = jnp.zeros((seq, c), BF16)
        edl[rd] = jnp.zeros((nch, SUBLANES, c), F32)

    def prologue(n):
        def conv_silu(u_ref, w_ref):
            u = u_ref[rows[n], :]
            w = w_ref[...]
            y = u * w[CONV_K - 1:CONV_K, :]
            for k in range(1, CONV_K):
                if n > 0:
                    sh = u_ref[n * c - k:(n + 1) * c - k, :]
                else:
                    sh = pltpu.roll(u, k, 0)
                    head = jnp.where(row8 < k, 0.0, sh[:SUBLANES, :])
                    sh = jnp.concatenate([head, sh[SUBLANES:, :]], axis=0)
                y = y + sh * w[CONV_K - 1 - k:CONV_K - k, :]
            return _silu(y)

        def l2n(x):
            return x * lax.rsqrt(jnp.sum(x * x, axis=-1, keepdims=True) + EPS)

        qs[wr, rows[n], :] = l2n(conv_silu(q_ref, cwq_ref)) * (GDN_DK ** -0.5)
        ks[wr, rows[n], :] = l2n(conv_silu(k_ref, cwk_ref))
        vs[wr, rows[n], :] = conv_silu(v_ref, cwv_ref)
        misc = misc_ref[rows[n], :]
        aa = jnp.sum(jnp.where(ci == hd, misc, 0.0), axis=-1, keepdims=True)
        ba = jnp.sum(jnp.where(ci == hd + GDN_HEADS, misc, 0.0), axis=-1, keepdims=True)
        bs[wr, rows[n], :] = jnp.broadcast_to(_sigmoid(ba), (c, c))
        g = -jnp.exp(alog_ref[0]) * _softplus(aa + dtb_ref[0])
        g1 = g.astype(BF16)
        r1 = g - g1.astype(F32)
        g2 = r1.astype(BF16)
        g3 = (r1 - g2.astype(F32)).astype(BF16)
        d = _dot(tri, jnp.concatenate([g1, g2, g3], axis=1))
        ds[wr, rows[n], :] = d[:, :c] + (d[:, c:2 * c] + d[:, 2 * c:])

    def recur(n, st):
        sb = st.astype(BF16)
        o = _dot(qws[rd, rows[n], :], sb) + ous[rd, rows[n], :]
        st = st * edl[rd, n][0:1, :] + (cus[rd, rows[n], :] - _dot(nws[rd, rows[n], :], sb))
        on = o * lax.rsqrt(jnp.mean(o * o, axis=-1, keepdims=True) + EPS) * ng
        o_ref[rows[n], :] = (on * _silu(z_ref[rows[n], :])).astype(o_ref.dtype)
        return st

    carry = [jnp.zeros((GDN_DK, GDN_DV), F32), 0]

    def hosted():
        n = carry[1]
        carry[0] = recur(n, carry[0])
        prologue(n)
        carry[1] = n + 1

    dcols = [ds[rd, r, :] for r in rows]
    ks_ = [ks[rd, r, :] for r in rows]
    kbs = [k * bs[rd, r, :] for k, r in zip(ks_, rows)]
    kq = [_dot_nt(jnp.concatenate([kb, qs[rd, r, :]], axis=0).astype(BF16), k.astype(BF16))
          for kb, k, r in zip(kbs, ks_, rows)]
    hosted()
    gammas = [jnp.exp(jnp.where(ri >= ci, d - d.T, -jnp.inf)) for d in dcols]
    a_list = [jnp.where(ri > ci, m[:c] * gm, 0.0) for m, gm in zip(kq, gammas)]
    attns = [(m[c:] * gm).astype(BF16) for m, gm in zip(kq, gammas)]
    ecols = [jnp.exp(d) for d in dcols]
    rhs = [jnp.concatenate([kb * e, vs[rd, r, :] * bs[rd, r, :]], axis=1)
           for r, kb, e in zip(rows, kbs, ecols)]
    hosted()
    tinvs = [t.astype(BF16) for t in _unit_lower_inverses(a_list, between=hosted)]
    x0 = [_dot(t, r.astype(BF16)) for t, r in zip(tinvs, rhs)]
    hosted()
    res = [r - (x + _mm3(a, x)) for r, x, a in zip(rhs, x0, a_list)]
    hosted()
    assert carry[1] == nch
    wu = [(x + _dot(t, e.astype(BF16))).astype(BF16) for x, t, e in zip(x0, tinvs, res)]
    kdts = [(k * jnp.exp(d[c - 1:c, :] - d)).T.astype(BF16) for k, d in zip(ks_, dcols)]
    aw = [_dot(at, x) for at, x in zip(attns, wu)]
    kw = [_dot(kd, x) for kd, x in zip(kdts, wu)]
    for n, (r, d, e, a_, k_) in enumerate(zip(rows, dcols, ecols, aw, kw)):
        qws[wr, r, :] = (qs[rd, r, :] * e - a_[:, :c]).astype(BF16)
        ous[wr, r, :] = a_[:, c:]
        nws[wr, r, :] = k_[:, :c].astype(BF16)
        cus[wr, r, :] = k_[:, c:]
        edl[wr, n] = jnp.broadcast_to(jnp.exp(d[c - 1:c, :]), (SUBLANES, c))


def _gdn(pa, pb, conv_w, a_log, dt_bias, gdn_norm_g, bsz, seq):
    nh = GDN_HEADS
    c = LANES
    nch = seq // c
    assert nch == 2 + 2 * (int(math.log2(c)) - 1) + 2
    n_items = bsz * nh
    alog_b = jnp.broadcast_to(a_log.reshape(nh, 1, 1), (nh, 1, c))
    dtb_b = jnp.broadcast_to(dt_bias.reshape(nh, 1, 1), (nh, 1, c))
    pro = lambda s: jnp.minimum(s, n_items - 1)
    rec = lambda s: jnp.clip(s - 2, 0, n_items - 1)
    blk = lambda off: pl.BlockSpec((seq, c), lambda s: (pro(s) // nh, off + pro(s) % nh))
    cw = lambda off: pl.BlockSpec((CONV_K, c), lambda s: (0, off + pro(s) % nh))
    hv = pl.BlockSpec((1, 1, c), lambda s: (pro(s) % nh, 0, 0))
    return pl.pallas_call(
        _gdn_kernel,
        grid=(n_items + 2,),
        in_specs=[blk(0), blk(nh), blk(2 * nh),
                  pl.BlockSpec((seq, c), lambda s: (pro(s) // nh, 2)),
                  cw(0), cw(nh), cw(2 * nh), hv, hv,
                  pl.BlockSpec((seq, c), lambda s: (rec(s) // nh, 3 * nh + rec(s) % nh)),
                  pl.BlockSpec((1, c), lambda s: (0, 0))],
        out_specs=pl.BlockSpec((seq, c), lambda s: (rec(s) // nh, rec(s) % nh)),
        out_shape=jax.ShapeDtypeStruct((bsz * seq, GDN_W), BF16),
        scratch_shapes=([pltpu.VMEM((2, seq, c), F32)] * 5
                        + [pltpu.VMEM((2, seq, c), BF16)] * 2
                        + [pltpu.VMEM((2, seq, c), F32)] * 2
                        + [pltpu.VMEM((2, nch, SUBLANES, c), F32)]),
        compiler_params=pltpu.CompilerParams(vmem_limit_bytes=VMEM_LIMIT),
        name="gated_deltanet",
    )(pa, pa, pa, pb, conv_w, conv_w, conv_w, alog_b, dtb_b, pa,
      gdn_norm_g.reshape(1, c))


def _t5_bucket_table(n_tiles):
    s = np.arange(LANES)[:, None]
    q = np.arange(LANES)[None, :]
    out = []
    max_exact = NUM_BUCKETS // 2
    for d in range(n_tiles):
        n = np.maximum(d * LANES + q - s, 0)
        nf = np.maximum(n, 1).astype(np.float32)
        large = max_exact + (np.log(nf / max_exact) / math.log(MAX_DISTANCE / max_exact)
                             * (NUM_BUCKETS - max_exact)).astype(np.int32)
        large = np.minimum(large, NUM_BUCKETS - 1)
        out.append(np.where(n < max_exact, n, large))
    return np.stack(out).astype(np.int32)


_BIAS_TILES = 3
assert (_BIAS_TILES - 1) * LANES - (LANES - 1) >= MAX_DISTANCE


def _bias_kernel(bkt_ref, rel_ref, o_ref):
    hd = pl.program_id(0)
    for d in range(_BIAS_TILES):
        bkt = bkt_ref[d]
        acc = jnp.zeros((LANES, LANES), F32)
        for j in range(NUM_BUCKETS):
            acc = jnp.where(bkt == j, rel_ref[j, hd] * LOG2E, acc)
        o_ref[d, 0] = acc


def _bias_tiles(rel_bias):
    bkt = jnp.asarray(_t5_bucket_table(_BIAS_TILES))
    return pl.pallas_call(
        _bias_kernel,
        grid=(DSA_HEADS,),
        in_specs=[pl.BlockSpec((_BIAS_TILES, LANES, LANES), lambda h: (0, 0, 0)),
                  pl.BlockSpec(memory_space=pltpu.SMEM)],
        out_specs=pl.BlockSpec((_BIAS_TILES, 1, LANES, LANES), lambda h: (0, h, 0, 0)),
        out_shape=jax.ShapeDtypeStruct((_BIAS_TILES, DSA_HEADS, LANES, LANES), F32),
        name="t5_bias_tiles",
    )(bkt, rel_bias)


def _dsa_kernel(qb_ref, qi_ref, misc_ref, zb_ref, kil_ref, kih_ref, ckv_ref,
                kvg_ref, wuk_ref, wuvt_ref, bias_ref, o_ref,
                ckvn, ckvt, keys, madd, qlat, lg, pbuf, acc):
    i = pl.program_id(1)
    c = LANES
    sc_w = DSA_SUPER * c
    nsc = i // DSA_SUPER + 1
    nh = DSA_HEADS
    seq = ckv_ref.shape[0]

    @pl.when(i == 0)
    def _():
        x = ckv_ref[...]
        y = x * lax.rsqrt(jnp.mean(x * x, axis=-1, keepdims=True) + EPS) * kvg_ref[...]
        ckvn[...] = y.astype(BF16)
        for r in range(seq // c):
            ckvt[r // DSA_SUPER, :KV_RANK, (r % DSA_SUPER) * c:(r % DSA_SUPER + 1) * c] = (
                y[r * c:(r + 1) * c, :].T.astype(BF16))
        ckvt[:, KV_RANK:, :] = jnp.ones((seq // sc_w, ONES_ROWS, sc_w), BF16)

    ri = lax.broadcasted_iota(jnp.int32, (c, c), 0)
    ci = lax.broadcasted_iota(jnp.int32, (c, c), 1)
    misc_t = misc_ref[...].T
    wscale = IDX_HEADS ** -0.5 * IDX_DIM ** -0.5
    wrow = [misc_t[2 * GDN_HEADS + h:2 * GDN_HEADS + h + 1, :] * wscale
            for h in range(IDX_HEADS)]
    npair = IDX_HEADS // 2
    qi4 = jnp.concatenate([qi_ref[:, p * c:(p + 1) * c] for p in range(npair)], axis=0)

    def score_super(s, carry):
        ccs = [s * DSA_SUPER + j for j in range(DSA_SUPER)]
        sls = [pl.ds(pl.multiple_of(cc * c, c), c) for cc in ccs]
        dots = [_dot_nt(jnp.concatenate([kil_ref[sl, :], kih_ref[sl, :]], axis=0), qi4)
                for sl in sls]
        for cc, d in zip(ccs, dots):
            acc_s = jnp.zeros((c, c), F32)
            for p in range(npair):
                acc_s = acc_s + wrow[2 * p] * jnp.maximum(d[:c, p * c:(p + 1) * c], 0.0)
                acc_s = acc_s + wrow[2 * p + 1] * jnp.maximum(d[c:, p * c:(p + 1) * c], 0.0)
            adm = (cc * c + ri) <= (i * c + ci)
            acc_s = jnp.where(adm, acc_s, -jnp.inf)
            bits = lax.bitcast_convert_type(acc_s, jnp.int32)
            keys[cc] = jnp.where(bits < 0, bits ^ 0x7FFFFFFF, bits)
        return carry

    lax.fori_loop(0, nsc, score_super, 0)

    kf = float(TOPK_MAX)

    def count(pred):
        def body(s, cnt):
            parts = [_fold_rows(jnp.where(pred(keys[s * DSA_SUPER + j]), 1.0, 0.0), jnp.add)
                     for j in range(DSA_SUPER)]
            return cnt + ((parts[0] + parts[1]) + (parts[2] + parts[3]))
        cnt = lax.fori_loop(0, nsc, body, jnp.zeros((SUBLANES, c), F32))
        return jnp.sum(cnt, axis=0, keepdims=True)

    def bit_step(it, carry):
        v, n_v = carry
        cand = v + jnp.left_shift(jnp.int32(1), 31 - it)
        tot = count(lambda kc: kc >= cand)
        take = tot >= kf
        return jnp.where(take, cand, v), jnp.where(take, tot, n_v)

    n_all = jnp.full((1, c), 1.0, F32) * (nsc * sc_w).astype(F32)
    thr, n_ge = lax.fori_loop(0, BISECT_CHECK, bit_step,
                              (jnp.full((1, c), INT_MIN, jnp.int32), n_all))
    n_gt = count(lambda kc: kc > thr)
    settled = jnp.min(jnp.where((n_ge == kf) | (n_gt < kf), 1.0, 0.0)) > 0.5

    def finish_bits(args):
        v, n_v, _ = args
        v, n_v = lax.fori_loop(BISECT_CHECK, 32, bit_step, (v, n_v))
        return v, n_v, count(lambda kc: kc > v)

    thr, n_ge, n_gt = lax.cond(settled, lambda args: args, finish_bits, (thr, n_ge, n_gt))
    need = kf - n_gt
    any_ties = jnp.max(n_ge - n_gt) > 1.0
    lower = (ci < ri).astype(BF16)

    def mask_super(s, running):
        ccs = [s * DSA_SUPER + j for j in range(DSA_SUPER)]
        kcs = [keys[cc] for cc in ccs]
        eqs = [jnp.where(kc == thr, 1.0, 0.0) for kc in kcs]
        prefix = [_dot(lower, eq.astype(BF16)) for eq in eqs]
        totals = [jnp.sum(_fold_rows(eq, jnp.add), axis=0, keepdims=True) for eq in eqs]
        for cc, kc, eq, pre, tot in zip(ccs, kcs, eqs, prefix, totals):
            sel = jnp.where(kc > thr, 1.0, jnp.where(running + pre < need, eq, 0.0))
            adm = (cc * c + ri) <= (i * c + ci)
            madd[cc] = jnp.where(adm & (sel > 0.5), 0.0, -jnp.inf)
            running = running + tot
        return running

    @pl.when(any_ties)
    def _():
        lax.fori_loop(0, nsc, mask_super, jnp.zeros((1, c), F32))

    @pl.when(jnp.logical_not(any_ties))
    def _():
        def mask_chunk(cc, carry):
            adm = (cc * c + ri) <= (i * c + ci)
            madd[cc] = jnp.where(adm & (keys[cc] >= thr), 0.0, -jnp.inf)
            return carry
        lax.fori_loop(0, nsc * DSA_SUPER, mask_chunk, 0)

    for h in range(nh):
        qh = qb_ref[:, h * DSA_DH:(h + 1) * DSA_DH]
        qlat[h * c:(h + 1) * c, :] = (
            _dot_nt(qh, wuk_ref[h]) * (DSA_DH ** -0.5 * LOG2E)).astype(BF16)

    def logit_super(s, mx):
        kv = ckvn[pl.ds(pl.multiple_of(s * sc_w, sc_w), sc_w), :]
        l_all = _dot_nt(kv, qlat[...])
        new = []
        for h in range(nh):
            mh = mx[:, h * c:(h + 1) * c]
            for j in range(DSA_SUPER):
                cc = s * DSA_SUPER + j
                bidx = jnp.clip(i - cc, 0, _BIAS_TILES - 1)
                l = (l_all[j * c:(j + 1) * c, h * c:(h + 1) * c]
                     + bias_ref[bidx, h] + madd[cc])
                lg[s, j * c:(j + 1) * c, h * c:(h + 1) * c] = l
                mh = jnp.maximum(mh, _fold_rows(l, jnp.maximum))
            new.append(mh)
        return jnp.concatenate(new, axis=1)

    mx = lax.fori_loop(0, nsc, logit_super,
                       jnp.full((SUBLANES, nh * c), -jnp.inf, F32))
    m = jnp.max(mx, axis=0, keepdims=True)

    def pv_products(s):
        for h in range(nh):
            mh = m[:, h * c:(h + 1) * c]
            for j in range(DSA_SUPER):
                p = jnp.exp2(lg[s, j * c:(j + 1) * c, h * c:(h + 1) * c] - mh)
                pbuf[j * c:(j + 1) * c, h * c:(h + 1) * c] = p.astype(BF16)
        return _dot(ckvt[s], pbuf[...])

    def pv_super(s, carry):
        acc[...] = acc[...] + pv_products(s)
        return carry

    acc[...] = pv_products(0)
    lax.fori_loop(1, nsc, pv_super, 0)
    denom = acc[KV_RANK:KV_RANK + 1, :]

    for h in range(nh):
        cols = slice(h * c, (h + 1) * c)
        olat_t = (acc[:KV_RANK, cols] / denom[:, cols]).astype(BF16)
        ob = _dot(wuvt_ref[h], olat_t).T
        zb = zb_ref[:, h * DSA_DH:(h + 1) * DSA_DH]
        o_ref[:, h * DSA_DH:(h + 1) * DSA_DH] = (ob * _silu(zb)).astype(o_ref.dtype)


def _dsa(pa, pb, pc, kv_norm_g, wuk_t, wuv_t, bias_tiles, bsz, seq):
    c = LANES
    nb = seq // c
    nsc = nb // DSA_SUPER
    rows_all = DSA_HEADS * c
    za_off = (4 * GDN_W + 2 * D_MODEL) // DSA_W
    qi_off = DSA_W // (IDX_HEADS * IDX_DIM)
    kil_off = (DSA_W + IDX_HEADS * IDX_DIM) // c
    row = lambda b, i: b * nb + i
    return pl.pallas_call(
        _dsa_kernel,
        grid=(bsz, nb),
        in_specs=[
            pl.BlockSpec((c, DSA_W), lambda b, i: (row(b, i), 0)),
            pl.BlockSpec((c, IDX_HEADS * IDX_DIM), lambda b, i: (row(b, i), qi_off)),
            pl.BlockSpec((c, c), lambda b, i: (row(b, i), 2)),
            pl.BlockSpec((c, DSA_W), lambda b, i: (row(b, i), za_off)),
            pl.BlockSpec((seq, c), lambda b, i: (b, kil_off)),
            pl.BlockSpec((seq, c), lambda b, i: (b, kil_off + 1)),
            pl.BlockSpec((seq, KV_RANK), lambda b, i: (b, 0)),
            pl.BlockSpec((1, KV_RANK), lambda b, i: (0, 0)),
            pl.BlockSpec((DSA_HEADS, KV_RANK, DSA_DH), lambda b, i: (0, 0, 0)),
            pl.BlockSpec((DSA_HEADS, DSA_DH, KV_RANK), lambda b, i: (0, 0, 0)),
            pl.BlockSpec((_BIAS_TILES, DSA_HEADS, c, c), lambda b, i: (0, 0, 0, 0)),
        ],
        out_specs=pl.BlockSpec((c, DSA_W), lambda b, i: (row(b, i), 0)),
        out_shape=jax.ShapeDtypeStruct((bsz * seq, DSA_W), BF16),
        scratch_shapes=[pltpu.VMEM((seq, KV_RANK), BF16),
                        pltpu.VMEM((nsc, KV_RANK + ONES_ROWS, DSA_SUPER * c), BF16),
                        pltpu.VMEM((nb, c, c), jnp.int32),
                        pltpu.VMEM((nb, c, c), F32),
                        pltpu.VMEM((rows_all, KV_RANK), BF16),
                        pltpu.VMEM((nsc, DSA_SUPER * c, rows_all), F32),
                        pltpu.VMEM((DSA_SUPER * c, rows_all), BF16),
                        pltpu.VMEM((KV_RANK + ONES_ROWS, rows_all), F32)],
        compiler_params=pltpu.CompilerParams(vmem_limit_bytes=VMEM_LIMIT),
        name="dsa_attention",
    )(pc, pc, pb, pa, pc, pc, pb, kv_norm_g.reshape(1, KV_RANK), wuk_t, wuv_t,
      bias_tiles)


def _out_kernel(oa_ref, ob_ref, gl_ref, x_ref, gate_ref, woa_ref, wob_ref, wout_ref,
                fg_ref, o_ref):
    d = x_ref.shape[1]
    ya = _dot(oa_ref[...], woa_ref[...])
    yb = _dot(ob_ref[...], wob_ref[...])
    gl = gl_ref[...]
    mix = _sigmoid(gl[:, :d]) * ya + _sigmoid(gl[:, d:]) * yb
    out = _dot(mix.astype(BF16), wout_ref[...])
    xn = x_ref[...] + gate_ref[0] * out
    y = xn * lax.rsqrt(jnp.mean(xn * xn, axis=-1, keepdims=True) + EPS)
    o_ref[...] = y * fg_ref[...]


def _merge_out(oa, ob, pa, x2, gate, w_o_gdn, w_o_dsa, w_out, final_g, bsz, seq):
    d = D_MODEL
    tm = 512
    nt = seq // tm
    gl_off = (4 * GDN_W) // (2 * d)
    row = lambda b, i: (b * nt + i, 0)
    full = lambda b, i: (0, 0)
    return pl.pallas_call(
        _out_kernel,
        grid=(bsz, nt),
        in_specs=[pl.BlockSpec((tm, GDN_W), row),
                  pl.BlockSpec((tm, DSA_W), row),
                  pl.BlockSpec((tm, 2 * d), lambda b, i: (b * nt + i, gl_off)),
                  pl.BlockSpec((tm, d), row),
                  pl.BlockSpec((1, 1, d), lambda b, i: (b, 0, 0)),
                  pl.BlockSpec((GDN_W, d), full),
                  pl.BlockSpec((DSA_W, d), full),
                  pl.BlockSpec((d, d), full),
                  pl.BlockSpec((1, d), full)],
        out_specs=pl.BlockSpec((tm, d), row),
        out_shape=jax.ShapeDtypeStruct((bsz * seq, d), F32),
        compiler_params=pltpu.CompilerParams(vmem_limit_bytes=VMEM_LIMIT),
        name="merge_out",
    )(oa, ob, pa, x2, gate.reshape(bsz, 1, d), w_o_gdn, w_o_dsa, w_out,
      final_g.reshape(1, d))


def _split_w_in(w):
    o = 0
    names = ("qa", "ka", "va", "za", "aa", "ba", "qb", "ckv", "zb", "qi", "ki", "wi", "gl")
    sizes = (GDN_HEADS * GDN_DK, GDN_HEADS * GDN_DK, GDN_W, GDN_W, GDN_HEADS, GDN_HEADS,
             DSA_W, KV_RANK, DSA_W, IDX_HEADS * IDX_DIM, IDX_DIM, IDX_HEADS, 2 * D_MODEL)
    w = w.astype(BF16)
    col = {}
    for nm, sz in zip(names, sizes):
        col[nm] = w[:, o:o + sz]
        o += sz
    k = w.shape[0]
    z = lambda n: jnp.zeros((k, n), w.dtype)
    wa = jnp.concatenate([col["qa"], col["ka"], col["va"], col["za"], col["gl"], col["zb"]], 1)
    n_misc = 2 * GDN_HEADS + IDX_HEADS
    wb = jnp.concatenate([col["ckv"], col["aa"], col["ba"], col["wi"], z(LANES - n_misc)], 1)
    pad = z(LANES - IDX_DIM)
    wc = jnp.concatenate([col["qb"], col["qi"], col["ki"], pad, pad, col["ki"]], 1)
    return wa, wb, wc


def kernel(x, c, norm_g, w_ada, b_ada, w_in, conv_w, a_log, dt_bias, gdn_norm_g, w_o_gdn,
           kv_norm_g, w_uk, w_uv, w_o_dsa, w_out, rel_bias, final_g):
    bsz, seq, d = x.shape
    depth = norm_g.shape[0]
    assert depth == 1, "the final RMSNorm is fused into the layer's merge kernel"
    x2 = x.reshape(bsz * seq, d)
    bias_tiles = _bias_tiles(rel_bias)
    for l in range(depth):
        mod = _adaln_mod(c, w_ada[l], b_ada[l])
        shift, scale, gate = mod[:, :d], mod[:, d:2 * d], mod[:, 2 * d:]
        h = _modulated_norm(x2, norm_g[l], scale, shift, seq)
        wa, wb, wc = _split_w_in(w_in[l])
        pa = _project(h, wa, F32, 1024, tm=2048)
        pb = _project(h, wb, F32, wb.shape[1], tm=2048)
        pc = _project(h, wc, BF16, wc.shape[1], tm=2048)
        oa = _gdn(pa, pb, conv_w[l], a_log[l], dt_bias[l], gdn_norm_g[l], bsz, seq)
        wuk_t = jnp.transpose(w_uk[l], (1, 0, 2)).astype(BF16)
        wuv_t = jnp.transpose(w_uv[l], (1, 2, 0)).astype(BF16)
        ob = _dsa(pa, pb, pc, kv_norm_g[l], wuk_t, wuv_t, bias_tiles, bsz, seq)
        x2 = _merge_out(oa, ob, pa, x2, gate, w_o_gdn[l].astype(BF16),
                        w_o_dsa[l].astype(BF16), w_out[l].astype(BF16), final_g,
                        bsz, seq)
    return x2.reshape(bsz, seq, d)
```

```python
import math

import numpy as np
import jax
import jax.numpy as jnp
from jax import lax
from jax.experimental import pallas as pl
from jax.experimental.pallas import tpu as pltpu

D_MODEL = 1024
GDN_HEADS = 8
GDN_DK = 128
GDN_DV = 128
CONV_K = 4
DSA_HEADS = 8
DSA_DH = 128
KV_RANK = 256
IDX_HEADS = 8
IDX_DIM = 64
TOPK_MAX = 256
NUM_BUCKETS = 32
MAX_DISTANCE = 128
EPS = 1e-6

LANES = 128
SUBLANES = 8
DSA_SUPER = 4
ONES_ROWS = 16
BISECT_CHECK = 25
GDN_W = GDN_HEADS * GDN_DV
DSA_W = DSA_HEADS * DSA_DH
LOG2E = math.log2(math.e)

F32 = jnp.float32
BF16 = jnp.bfloat16
INT_MIN = -2 ** 31
VMEM_LIMIT = 48 * 1024 * 1024

_NT = (((1,), (1,)), ((), ()))


def _dot(a, b):
    return jnp.dot(a, b, preferred_element_type=F32)


def _dot_nt(a, b):
    return lax.dot_general(a, b, _NT, preferred_element_type=F32)


def _split2(a):
    hi = a.astype(BF16)
    lo = (a - hi.astype(F32)).astype(BF16)
    return hi, lo


def _mm3(a, b):
    ah, al = _split2(a)
    bh, bl = _split2(b)
    n = b.shape[1]
    big = _dot(ah, jnp.concatenate([bh, bl], axis=1))
    return big[:, :n] + (big[:, n:] + _dot(al, bh))


def _sigmoid(x):
    return 0.5 * jnp.tanh(0.5 * x) + 0.5


def _silu(x):
    h = 0.5 * x
    return h * jnp.tanh(h) + h


def _softplus(x):
    return jnp.maximum(x, 0.0) + jnp.log1p(jnp.exp(-jnp.abs(x)))


def _fold_rows(x, op):
    parts = [x[r:r + SUBLANES, :] for r in range(0, x.shape[0], SUBLANES)]
    while len(parts) > 1:
        nxt = [op(parts[j], parts[j + 1]) for j in range(0, len(parts) - 1, 2)]
        if len(parts) % 2:
            nxt.append(parts[-1])
        parts = nxt
    return parts[0]


def _mod_kernel(c_ref, w_ref, b_ref, o_ref):
    a = _silu(c_ref[...])
    o_ref[...] = _mm3(a, w_ref[...]) + b_ref[...]


def _adaln_mod(c, w_ada, b_ada):
    bsz, d = c.shape
    n = w_ada.shape[1]
    tn = 512
    return pl.pallas_call(
        _mod_kernel,
        grid=(n // tn,),
        in_specs=[pl.BlockSpec((bsz, d), lambda j: (0, 0)),
                  pl.BlockSpec((d, tn), lambda j: (0, j)),
                  pl.BlockSpec((1, tn), lambda j: (0, j))],
        out_specs=pl.BlockSpec((bsz, tn), lambda j: (0, j)),
        out_shape=jax.ShapeDtypeStruct((bsz, n), F32),
        name="adaln_mod",
    )(c, w_ada, b_ada.reshape(1, n))


def _h_kernel(x_ref, g_ref, sc_ref, sh_ref, wb_ref, wc_ref, h_ref, pb_ref, pc_ref):
    x = x_ref[...]
    y = x * lax.rsqrt(jnp.mean(x * x, axis=-1, keepdims=True) + EPS)
    y = y * g_ref[...]
    h = (y * (1.0 + sc_ref[0]) + sh_ref[0]).astype(h_ref.dtype)
    h_ref[...] = h
    pb_ref[...] = _dot(h, wb_ref[...])
    pc_ref[...] = _dot(h, wc_ref[...]).astype(pc_ref.dtype)


def _modulated_norm_proj(x2, norm_g, scale, shift, wb, wc, seq):
    m, d = x2.shape
    bsz = m // seq
    tm = 1024
    nt = seq // tm
    nb_, nc_ = wb.shape[1], wc.shape[1]
    row = lambda b, i: (b * nt + i, 0)
    full = lambda b, i: (0, 0)
    return pl.pallas_call(
        _h_kernel,
        grid=(bsz, nt),
        in_specs=[pl.BlockSpec((tm, d), row),
                  pl.BlockSpec((1, d), full),
                  pl.BlockSpec((1, 1, d), lambda b, i: (b, 0, 0)),
                  pl.BlockSpec((1, 1, d), lambda b, i: (b, 0, 0)),
                  pl.BlockSpec((d, nb_), full),
                  pl.BlockSpec((d, nc_), full)],
        out_specs=[pl.BlockSpec((tm, d), row),
                   pl.BlockSpec((tm, nb_), row),
                   pl.BlockSpec((tm, nc_), row)],
        out_shape=[jax.ShapeDtypeStruct((m, d), BF16),
                   jax.ShapeDtypeStruct((m, nb_), F32),
                   jax.ShapeDtypeStruct((m, nc_), BF16)],
        compiler_params=pltpu.CompilerParams(vmem_limit_bytes=VMEM_LIMIT),
        name="modulated_norm_proj",
    )(x2, norm_g.reshape(1, d), scale.reshape(bsz, 1, d), shift.reshape(bsz, 1, d), wb, wc)


def _proj_kernel(h_ref, w_ref, o_ref):
    o_ref[...] = _dot(h_ref[...], w_ref[...]).astype(o_ref.dtype)


def _project(h, w, out_dtype, tn, tm=1024):
    m, k = h.shape
    n = w.shape[1]
    return pl.pallas_call(
        _proj_kernel,
        grid=(m // tm, n // tn),
        in_specs=[pl.BlockSpec((tm, k), lambda i, j: (i, 0)),
                  pl.BlockSpec((k, tn), lambda i, j: (0, j))],
        out_specs=pl.BlockSpec((tm, tn), lambda i, j: (i, j)),
        out_shape=jax.ShapeDtypeStruct((m, n), out_dtype),
        compiler_params=pltpu.CompilerParams(vmem_limit_bytes=VMEM_LIMIT),
        name="in_proj",
    )(h, w)


def _unit_lower_inverses(a_list, between=None):
    n = a_list[0].shape[0]
    ri = lax.broadcasted_iota(jnp.int32, (n, n), 0)
    ci = lax.broadcasted_iota(jnp.int32, (n, n), 1)
    x = ri ^ ci
    eye = (ri == ci).astype(F32)
    ts = [eye - jnp.where(x == 1, a, 0.0) for a in a_list]
    s = 2
    while s < n:
        level = (x >= s) & (x < 2 * s)
        tb = [t.astype(BF16) for t in ts]
        ps = [_dot(jnp.where(level, a, 0.0).astype(BF16), t) for a, t in zip(a_list, tb)]
        if between is not None:
            between()
        ts = [t - _dot(t16, p.astype(BF16)) for t, t16, p in zip(ts, tb, ps)]
        if between is not None:
            between()
        s *= 2
    return ts


def _gdn_kernel(q_ref, k_ref, v_ref, misc_ref, cwq_ref, cwk_ref, cwv_ref,
                alog_ref, dtb_ref, z_ref, ng_ref, o_ref, qs, ks, vs, ds, bs,
                qws, nws, cus, ous, edl):
    s = pl.program_id(0)
    n_items = pl.num_programs(0) - 2
    seq = q_ref.shape[0]
    c = LANES
    nch = seq // c
    wr = s % 2
    rd = (s + 1) % 2
    hd = jnp.minimum(s, n_items - 1) % GDN_HEADS
    row8 = lax.broadcasted_iota(jnp.int32, (SUBLANES, c), 0)
    ri = lax.broadcasted_iota(jnp.int32, (c, c), 0)
    ci = lax.broadcasted_iota(jnp.int32, (c, c), 1)
    tri = (ri >= ci).astype(BF16)
    ng = ng_ref[...]
    rows = [slice(n * c, (n + 1) * c) for n in range(nch)]

    @pl.when(s == 0)
    def _():
        for ref in (qs, ks, vs, ds, bs, cus, ous):
            ref[rd] = jnp.zeros((seq, c), F32)
        for ref in (qws, nws):
            ref[rd] = jnp.zeros((seq, c), BF16)
        edl[rd] = jnp.zeros((nch, SUBLANES, c), F32)

    def prologue(n):
        def conv_silu(u_ref, w_ref):
            u = u_ref[rows[n], :]
            w = w_ref[...]
            y = u * w[CONV_K - 1:CONV_K, :]
            for k in range(1, CONV_K):
                if n > 0:
                    sh = u_ref[n * c - k:(n + 1) * c - k, :]
                else:
                    sh = pltpu.roll(u, k, 0)
                    head = jnp.where(row8 < k, 0.0, sh[:SUBLANES, :])
                    sh = jnp.concatenate([head, sh[SUBLANES:, :]], axis=0)
                y = y + sh * w[CONV_K - 1 - k:CONV_K - k, :]
            return _silu(y)

        def l2n(x):
            return x * lax.rsqrt(jnp.sum(x * x, axis=-1, keepdims=True) + EPS)

        qs[wr, rows[n], :] = l2n(conv_silu(q_ref, cwq_ref)) * (GDN_DK ** -0.5)
        ks[wr, rows[n], :] = l2n(conv_silu(k_ref, cwk_ref))
        vs[wr, rows[n], :] = conv_silu(v_ref, cwv_ref)
        misc = misc_ref[rows[n], :]
        aa = jnp.sum(jnp.where(ci == hd, misc, 0.0), axis=-1, keepdims=True)
        ba = jnp.sum(jnp.where(ci == hd + GDN_HEADS, misc, 0.0), axis=-1, keepdims=True)
        bs[wr, rows[n], :] = jnp.broadcast_to(_sigmoid(ba), (c, c))
        g = -jnp.exp(alog_ref[0]) * _softplus(aa + dtb_ref[0])
        g1 = g.astype(BF16)
        r1 = g - g1.astype(F32)
        g2 = r1.astype(BF16)
        g3 = (r1 - g2.astype(F32)).astype(BF16)
        d = _dot(tri, jnp.concatenate([g1, g2, g3], axis=1))
        ds[wr, rows[n], :] = d[:, :c] + (d[:, c:2 * c] + d[:, 2 * c:])

    def recur(n, st):
        sb = st.astype(BF16)
        o = _dot(qws[rd, rows[n], :], sb) + ous[rd, rows[n], :]
        st = st * edl[rd, n][0:1, :] + (cus[rd, rows[n], :] - _dot(nws[rd, rows[n], :], sb))
        on = o * lax.rsqrt(jnp.mean(o * o, axis=-1, keepdims=True) + EPS) * ng
        o_ref[rows[n], :] = (on * _silu(z_ref[rows[n], :])).astype(o_ref.dtype)
        return st

    carry = [jnp.zeros((GDN_DK, GDN_DV), F32), 0]

    def hosted():
        n = carry[1]
        carry[0] = recur(n, carry[0])
        prologue(n)
        carry[1] = n + 1

    dcols = [ds[rd, r, :] for r in rows]
    ks_ = [ks[rd, r, :] for r in rows]
    kbs = [k * bs[rd, r, :] for k, r in zip(ks_, rows)]
    kq = [_dot_nt(jnp.concatenate([kb, qs[rd, r, :]], axis=0).astype(BF16), k.astype(BF16))
          for kb, k, r in zip(kbs, ks_, rows)]
    hosted()
    gammas = [jnp.exp(jnp.where(ri >= ci, d - d.T, -jnp.inf)) for d in dcols]
    a_list = [jnp.where(ri > ci, m[:c] * gm, 0.0) for m, gm in zip(kq, gammas)]
    attns = [(m[c:] * gm).astype(BF16) for m, gm in zip(kq, gammas)]
    ecols = [jnp.exp(d) for d in dcols]
    rhs = [jnp.concatenate([kb * e, vs[rd, r, :] * bs[rd, r, :]], axis=1)
           for r, kb, e in zip(rows, kbs, ecols)]
    hosted()
    tinvs = [t.astype(BF16) for t in _unit_lower_inverses(a_list, between=hosted)]
    x0 = [_dot(t, r.astype(BF16)) for t, r in zip(tinvs, rhs)]
    hosted()
    res = [r - (x + _mm3(a, x)) for r, x, a in zip(rhs, x0, a_list)]
    hosted()
    assert carry[1] == nch
    wu = [(x + _dot(t, e.astype(BF16))).astype(BF16) for x, t, e in zip(x0, tinvs, res)]
    kdts = [(k * jnp.exp(d[c - 1:c, :] - d)).T.astype(BF16) for k, d in zip(ks_, dcols)]
    aw = [_dot(at, x) for at, x in zip(attns, wu)]
    kw = [_dot(kd, x) for kd, x in zip(kdts, wu)]
    for n, (r, d, e, a_, k_) in enumerate(zip(rows, dcols, ecols, aw, kw)):
        qws[wr, r, :] = (qs[rd, r, :] * e - a_[:, :c]).astype(BF16)
        ous[wr, r, :] = a_[:, c:]
        nws[wr, r, :] = k_[:, :c].astype(BF16)
        cus[wr, r, :] = k_[:, c:]
        edl[wr, n] = jnp.broadcast_to(jnp.exp(d[c - 1:c, :]), (SUBLANES, c))


def _gdn(pa, pb, conv_w, a_log, dt_bias, gdn_norm_g, bsz, seq):
    nh = GDN_HEADS
    c = LANES
    nch = seq // c
    assert nch == 2 + 2 * (int(math.log2(c)) - 1) + 2
    n_items = bsz * nh
    alog_b = jnp.broadcast_to(a_log.reshape(nh, 1, 1), (nh, 1, c))
    dtb_b = jnp.broadcast_to(dt_bias.reshape(nh, 1, 1), (nh, 1, c))
    pro = lambda s: jnp.minimum(s, n_items - 1)
    rec = lambda s: jnp.clip(s - 2, 0, n_items - 1)
    blk = lambda off: pl.BlockSpec((seq, c), lambda s: (pro(s) // nh, off + pro(s) % nh))
    cw = lambda off: pl.BlockSpec((CONV_K, c), lambda s: (0, off + pro(s) % nh))
    hv = pl.BlockSpec((1, 1, c), lambda s: (pro(s) % nh, 0, 0))
    return pl.pallas_call(
        _gdn_kernel,
        grid=(n_items + 2,),
        in_specs=[blk(0), blk(nh), blk(2 * nh),
                  pl.BlockSpec((seq, c), lambda s: (pro(s) // nh, 2)),
                  cw(0), cw(nh), cw(2 * nh), hv, hv,
                  pl.BlockSpec((seq, c), lambda s: (rec(s) // nh, 3 * nh + rec(s) % nh)),
                  pl.BlockSpec((1, c), lambda s: (0, 0))],
        out_specs=pl.BlockSpec((seq, c), lambda s: (rec(s) // nh, rec(s) % nh)),
        out_shape=jax.ShapeDtypeStruct((bsz * seq, GDN_W), BF16),
        scratch_shapes=([pltpu.VMEM((2, seq, c), F32)] * 5
                        + [pltpu.VMEM((2, seq, c), BF16)] * 2
                        + [pltpu.VMEM((2, seq, c), F32)] * 2
                        + [pltpu.VMEM((2, nch, SUBLANES, c), F32)]),
        compiler_params=pltpu.CompilerParams(vmem_limit_bytes=VMEM_LIMIT),
        name="gated_deltanet",
    )(pa, pa, pa, pb, conv_w, conv_w, conv_w, alog_b, dtb_b, pa,
      gdn_norm_g.reshape(1, c))


def _t5_bucket_table(n_tiles):
    s = np.arange(LANES)[:, None]
    q = np.arange(LANES)[None, :]
    out = []
    max_exact = NUM_BUCKETS // 2
    for d in range(n_tiles):
        n = np.maximum(d * LANES + q - s, 0)
        nf = np.maximum(n, 1).astype(np.float32)
        large = max_exact + (np.log(nf / max_exact) / math.log(MAX_DISTANCE / max_exact)
                             * (NUM_BUCKETS - max_exact)).astype(np.int32)
        large = np.minimum(large, NUM_BUCKETS - 1)
        out.append(np.where(n < max_exact, n, large))
    return np.stack(out).astype(np.int32)


_BIAS_TILES = 3
assert (_BIAS_TILES - 1) * LANES - (LANES - 1) >= MAX_DISTANCE


def _bias_kernel(bkt_ref, rel_ref, o_ref):
    hd = pl.program_id(0)
    for d in range(_BIAS_TILES):
        bkt = bkt_ref[d]
        acc = jnp.zeros((LANES, LANES), F32)
        for j in range(NUM_BUCKETS):
            acc = jnp.where(bkt == j, rel_ref[j, hd] * LOG2E, acc)
        o_ref[d, 0] = acc


def _bias_tiles(rel_bias):
    bkt = jnp.asarray(_t5_bucket_table(_BIAS_TILES))
    return pl.pallas_call(
        _bias_kernel,
        grid=(DSA_HEADS,),
        in_specs=[pl.BlockSpec((_BIAS_TILES, LANES, LANES), lambda h: (0, 0, 0)),
                  pl.BlockSpec(memory_space=pltpu.SMEM)],
        out_specs=pl.BlockSpec((_BIAS_TILES, 1, LANES, LANES), lambda h: (0, h, 0, 0)),
        out_shape=jax.ShapeDtypeStruct((_BIAS_TILES, DSA_HEADS, LANES, LANES), F32),
        name="t5_bias_tiles",
    )(bkt, rel_bias)


def _dsa_kernel(qb_ref, qi_ref, misc_ref, zb_ref, kil_ref, kih_ref, ckv_ref,
                kvg_ref, wuk_ref, wuvt_ref, bias_ref, o_ref,
                ckvn, ckvt, keys, madd, qlat, lg, pbuf, acc):
    i = pl.program_id(1)
    c = LANES
    sc_w = DSA_SUPER * c
    nsc = i // DSA_SUPER + 1
    nh = DSA_HEADS
    seq = ckv_ref.shape[0]

    @pl.when(i == 0)
    def _():
        x = ckv_ref[...]
        y = x * lax.rsqrt(jnp.mean(x * x, axis=-1, keepdims=True) + EPS) * kvg_ref[...]
        ckvn[...] = y.astype(BF16)
        for r in range(seq // c):
            ckvt[r // DSA_SUPER, :KV_RANK, (r % DSA_SUPER) * c:(r % DSA_SUPER + 1) * c] = (
                y[r * c:(r + 1) * c, :].T.astype(BF16))
        ckvt[:, KV_RANK:, :] = jnp.ones((seq // sc_w, ONES_ROWS, sc_w), BF16)

    ri = lax.broadcasted_iota(jnp.int32, (c, c), 0)
    ci = lax.broadcasted_iota(jnp.int32, (c, c), 1)
    misc_t = misc_ref[...].T
    wscale = IDX_HEADS ** -0.5 * IDX_DIM ** -0.5
    wrow = [misc_t[2 * GDN_HEADS + h:2 * GDN_HEADS + h + 1, :] * wscale
            for h in range(IDX_HEADS)]
    npair = IDX_HEADS // 2
    qi4 = jnp.concatenate([qi_ref[:, p * c:(p + 1) * c] for p in range(npair)], axis=0)

    def score_super(s, carry):
        ccs = [s * DSA_SUPER + j for j in range(DSA_SUPER)]
        sls = [pl.ds(pl.multiple_of(cc * c, c), c) for cc in ccs]
        dots = [_dot_nt(jnp.concatenate([kil_ref[sl, :], kih_ref[sl, :]], axis=0), qi4)
                for sl in sls]
        for cc, d in zip(ccs, dots):
            acc_s = jnp.zeros((c, c), F32)
            for p in range(npair):
                acc_s = acc_s + wrow[2 * p] * jnp.maximum(d[:c, p * c:(p + 1) * c], 0.0)
                acc_s = acc_s + wrow[2 * p + 1] * jnp.maximum(d[c:, p * c:(p + 1) * c], 0.0)
            adm = (cc * c + ri) <= (i * c + ci)
            acc_s = jnp.where(adm, acc_s, -jnp.inf)
            bits = lax.bitcast_convert_type(acc_s, jnp.int32)
            keys[cc] = jnp.where(bits < 0, bits ^ 0x7FFFFFFF, bits)
        return carry

    lax.fori_loop(0, nsc, score_super, 0)

    kf = float(TOPK_MAX)

    def count(pred):
        def body(s, cnt):
            parts = [_fold_rows(jnp.where(pred(keys[s * DSA_SUPER + j]), 1.0, 0.0), jnp.add)
                     for j in range(DSA_SUPER)]
            return cnt + ((parts[0] + parts[1]) + (parts[2] + parts[3]))
        cnt = lax.fori_loop(0, nsc, body, jnp.zeros((SUBLANES, c), F32))
        return jnp.sum(cnt, axis=0, keepdims=True)

    def bit_step(it, carry):
        v, n_v = carry
        cand = v + jnp.left_shift(jnp.int32(1), 31 - it)
        tot = count(lambda kc: kc >= cand)
        take = tot >= kf
        return jnp.where(take, cand, v), jnp.where(take, tot, n_v)

    n_all = jnp.full((1, c), 1.0, F32) * (nsc * sc_w).astype(F32)
    thr, n_ge = lax.fori_loop(0, BISECT_CHECK, bit_step,
                              (jnp.full((1, c), INT_MIN, jnp.int32), n_all))
    n_gt = count(lambda kc: kc > thr)
    settled = jnp.min(jnp.where((n_ge == kf) | (n_gt < kf), 1.0, 0.0)) > 0.5

    def finish_bits(args):
        v, n_v, _ = args
        v, n_v = lax.fori_loop(BISECT_CHECK, 32, bit_step, (v, n_v))
        return v, n_v, count(lambda kc: kc > v)

    thr, n_ge, n_gt = lax.cond(settled, lambda args: args, finish_bits, (thr, n_ge, n_gt))
    need = kf - n_gt
    any_ties = jnp.max(n_ge - n_gt) > 1.0
    lower = (ci < ri).astype(BF16)

    def mask_super(s, running):
        ccs = [s * DSA_SUPER + j for j in range(DSA_SUPER)]
        kcs = [keys[cc] for cc in ccs]
        eqs = [jnp.where(kc == thr, 1.0, 0.0) for kc in kcs]
        prefix = [_dot(lower, eq.astype(BF16)) for eq in eqs]
        totals = [jnp.sum(_fold_rows(eq, jnp.add), axis=0, keepdims=True) for eq in eqs]
        for cc, kc, eq, pre, tot in zip(ccs, kcs, eqs, prefix, totals):
            sel = jnp.where(kc > thr, 1.0, jnp.where(running + pre < need, eq, 0.0))
            adm = (cc * c + ri) <= (i * c + ci)
            madd[cc] = jnp.where(adm & (sel > 0.5), 0.0, -jnp.inf)
            running = running + tot
        return running

    @pl.when(any_ties)
    def _():
        lax.fori_loop(0, nsc, mask_super, jnp.zeros((1, c), F32))

    @pl.when(jnp.logical_not(any_ties))
    def _():
        def mask_chunk(cc, carry):
            adm = (cc * c + ri) <= (i * c + ci)
            madd[cc] = jnp.where(adm & (keys[cc] >= thr), 0.0, -jnp.inf)
            return carry
        lax.fori_loop(0, nsc * DSA_SUPER, mask_chunk, 0)

    for h in range(nh):
        qh = qb_ref[:, h * DSA_DH:(h + 1) * DSA_DH]
        qlat[h * c:(h + 1) * c, :] = (
            _dot_nt(qh, wuk_ref[h]) * (DSA_DH ** -0.5 * LOG2E)).astype(BF16)

    def logit_super(s, mx):
        kv = ckvn[pl.ds(pl.multiple_of(s * sc_w, sc_w), sc_w), :]
        l_all = _dot_nt(kv, qlat[...])
        new = []
        for h in range(nh):
            mh = mx[:, h * c:(h + 1) * c]
            for j in range(DSA_SUPER):
                cc = s * DSA_SUPER + j
                bidx = jnp.clip(i - cc, 0, _BIAS_TILES - 1)
                l = (l_all[j * c:(j + 1) * c, h * c:(h + 1) * c]
                     + bias_ref[bidx, h] + madd[cc])
                lg[s, j * c:(j + 1) * c, h * c:(h + 1) * c] = l
                mh = jnp.maximum(mh, _fold_rows(l, jnp.maximum))
            new.append(mh)
        return jnp.concatenate(new, axis=1)

    mx = lax.fori_loop(0, nsc, logit_super,
                       jnp.full((SUBLANES, nh * c), -jnp.inf, F32))
    m = jnp.max(mx, axis=0, keepdims=True)

    def pv_products(s):
        for h in range(nh):
            mh = m[:, h * c:(h + 1) * c]
            for j in range(DSA_SUPER):
                p = jnp.exp2(lg[s, j * c:(j + 1) * c, h * c:(h + 1) * c] - mh)
                pbuf[j * c:(j + 1) * c, h * c:(h + 1) * c] = p.astype(BF16)
        return _dot(ckvt[s], pbuf[...])

    def pv_super(s, carry):
        acc[...] = acc[...] + pv_products(s)
        return carry

    acc[...] = pv_products(0)
    lax.fori_loop(1, nsc, pv_super, 0)
    denom = acc[KV_RANK:KV_RANK + 1, :]

    for h in range(nh):
        cols = slice(h * c, (h + 1) * c)
        olat_t = (acc[:KV_RANK, cols] / denom[:, cols]).astype(BF16)
        ob = _dot(wuvt_ref[h], olat_t).T
        zb = zb_ref[:, h * DSA_DH:(h + 1) * DSA_DH]
        o_ref[:, h * DSA_DH:(h + 1) * DSA_DH] = (ob * _silu(zb)).astype(o_ref.dtype)


def _dsa(pa, pb, pc, kv_norm_g, wuk_t, wuv_t, bias_tiles, bsz, seq):
    c = LANES
    nb = seq // c
    nsc = nb // DSA_SUPER
    rows_all = DSA_HEADS * c
    za_off = (4 * GDN_W + 2 * D_MODEL) // DSA_W
    qi_off = DSA_W // (IDX_HEADS * IDX_DIM)
    kil_off = (DSA_W + IDX_HEADS * IDX_DIM) // c
    row = lambda b, i: b * nb + i
    return pl.pallas_call(
        _dsa_kernel,
        grid=(bsz, nb),
        in_specs=[
            pl.BlockSpec((c, DSA_W), lambda b, i: (row(b, i), 0)),
            pl.BlockSpec((c, IDX_HEADS * IDX_DIM), lambda b, i: (row(b, i), qi_off)),
            pl.BlockSpec((c, c), lambda b, i: (row(b, i), 2)),
            pl.BlockSpec((c, DSA_W), lambda b, i: (row(b, i), za_off)),
            pl.BlockSpec((seq, c), lambda b, i: (b, kil_off)),
            pl.BlockSpec((seq, c), lambda b, i: (b, kil_off + 1)),
            pl.BlockSpec((seq, KV_RANK), lambda b, i: (b, 0)),
            pl.BlockSpec((1, KV_RANK), lambda b, i: (0, 0)),
            pl.BlockSpec((DSA_HEADS, KV_RANK, DSA_DH), lambda b, i: (0, 0, 0)),
            pl.BlockSpec((DSA_HEADS, DSA_DH, KV_RANK), lambda b, i: (0, 0, 0)),
            pl.BlockSpec((_BIAS_TILES, DSA_HEADS, c, c), lambda b, i: (0, 0, 0, 0)),
        ],
        out_specs=pl.BlockSpec((c, DSA_W), lambda b, i: (row(b, i), 0)),
        out_shape=jax.ShapeDtypeStruct((bsz * seq, DSA_W), BF16),
        scratch_shapes=[pltpu.VMEM((seq, KV_RANK), BF16),
                        pltpu.VMEM((nsc, KV_RANK + ONES_ROWS, DSA_SUPER * c), BF16),
                        pltpu.VMEM((nb, c, c), jnp.int32),
                        pltpu.VMEM((nb, c, c), F32),
                        pltpu.VMEM((rows_all, KV_RANK), BF16),
                        pltpu.VMEM((nsc, DSA_SUPER * c, rows_all), F32),
                        pltpu.VMEM((DSA_SUPER * c, rows_all), BF16),
                        pltpu.VMEM((KV_RANK + ONES_ROWS, rows_all), F32)],
        compiler_params=pltpu.CompilerParams(vmem_limit_bytes=VMEM_LIMIT),
        name="dsa_attention",
    )(pc, pc, pb, pa, pc, pc, pb, kv_norm_g.reshape(1, KV_RANK), wuk_t, wuv_t,
      bias_tiles)


def _out_kernel(oa_ref, ob_ref, gl_ref, x_ref, gate_ref, woa_ref, wob_ref, wout_ref,
                fg_ref, o_ref):
    d = x_ref.shape[1]
    ya = _dot(oa_ref[...], woa_ref[...])
    yb = _dot(ob_ref[...], wob_ref[...])
    gl = gl_ref[...]
    mix = _sigmoid(gl[:, :d]) * ya + _sigmoid(gl[:, d:]) * yb
    out = _dot(mix.astype(BF16), wout_ref[...])
    xn = x_ref[...] + gate_ref[0] * out
    y = xn * lax.rsqrt(jnp.mean(xn * xn, axis=-1, keepdims=True) + EPS)
    o_ref[...] = y * fg_ref[...]


def _merge_out(oa, ob, pa, x2, gate, w_o_gdn, w_o_dsa, w_out, final_g, bsz, seq):
    d = D_MODEL
    tm = 512
    nt = seq // tm
    gl_off = (4 * GDN_W) // (2 * d)
    row = lambda b, i: (b * nt + i, 0)
    full = lambda b, i: (0, 0)
    return pl.pallas_call(
        _out_kernel,
        grid=(bsz, nt),
        in_specs=[pl.BlockSpec((tm, GDN_W), row),
                  pl.BlockSpec((tm, DSA_W), row),
                  pl.BlockSpec((tm, 2 * d), lambda b, i: (b * nt + i, gl_off)),
                  pl.BlockSpec((tm, d), row),
                  pl.BlockSpec((1, 1, d), lambda b, i: (b, 0, 0)),
                  pl.BlockSpec((GDN_W, d), full),
                  pl.BlockSpec((DSA_W, d), full),
                  pl.BlockSpec((d, d), full),
                  pl.BlockSpec((1, d), full)],
        out_specs=pl.BlockSpec((tm, d), row),
        out_shape=jax.ShapeDtypeStruct((bsz * seq, d), F32),
        compiler_params=pltpu.CompilerParams(vmem_limit_bytes=VMEM_LIMIT),
        name="merge_out",
    )(oa, ob, pa, x2, gate.reshape(bsz, 1, d), w_o_gdn, w_o_dsa, w_out,
      final_g.reshape(1, d))


def _split_w_in(w):
    o = 0
    names = ("qa", "ka", "va", "za", "aa", "ba", "qb", "ckv", "zb", "qi", "ki", "wi", "gl")
    sizes = (GDN_HEADS * GDN_DK, GDN_HEADS * GDN_DK, GDN_W, GDN_W, GDN_HEADS, GDN_HEADS,
             DSA_W, KV_RANK, DSA_W, IDX_HEADS * IDX_DIM, IDX_DIM, IDX_HEADS, 2 * D_MODEL)
    w = w.astype(BF16)
    col = {}
    for nm, sz in zip(names, sizes):
        col[nm] = w[:, o:o + sz]
        o += sz
    k = w.shape[0]
    z = lambda n: jnp.zeros((k, n), w.dtype)
    wa = jnp.concatenate([col["qa"], col["ka"], col["va"], col["za"], col["gl"], col["zb"]], 1)
    n_misc = 2 * GDN_HEADS + IDX_HEADS
    wb = jnp.concatenate([col["ckv"], col["aa"], col["ba"], col["wi"], z(LANES - n_misc)], 1)
    pad = z(LANES - IDX_DIM)
    wc = jnp.concatenate([col["qb"], col["qi"], col["ki"], pad, pad, col["ki"]], 1)
    return wa, wb, wc


def kernel(x, c, norm_g, w_ada, b_ada, w_in, conv_w, a_log, dt_bias, gdn_norm_g, w_o_gdn,
           kv_norm_g, w_uk, w_uv, w_o_dsa, w_out, rel_bias, final_g):
    bsz, seq, d = x.shape
    depth = norm_g.shape[0]
    assert depth == 1, "the final RMSNorm is fused into the layer's merge kernel"
    x2 = x.reshape(bsz * seq, d)
    bias_tiles = _bias_tiles(rel_bias)
    for l in range(depth):
        mod = _adaln_mod(c, w_ada[l], b_ada[l])
        shift, scale, gate = mod[:, :d], mod[:, d:2 * d], mod[:, 2 * d:]
        wa, wb, wc = _split_w_in(w_in[l])
        h, pb, pc = _modulated_norm_proj(x2, norm_g[l], scale, shift, wb, wc, seq)
        pa = _project(h, wa, F32, 1024, tm=2048)
        oa = _gdn(pa, pb, conv_w[l], a_log[l], dt_bias[l], gdn_norm_g[l], bsz, seq)
        wuk_t = jnp.transpose(w_uk[l], (1, 0, 2)).astype(BF16)
        wuv_t = jnp.transpose(w_uv[l], (1, 2, 0)).astype(BF16)
        ob = _dsa(pa, pb, pc, kv_norm_g[l], wuk_t, wuv_t, bias_tiles, bsz, seq)
        x2 = _merge_out(oa, ob, pa, x2, gate, w_o_gdn[l].astype(BF16),
                        w_o_dsa[l].astype(BF16), w_out[l].astype(BF16), final_g,
                        bsz, seq)
    return x2.reshape(bsz, seq, d)
```
